```python
import jax, jax.numpy as jnp
from jax import lax
import numpy as np

D_MODEL = 1024
BATCH = 4
SEQ = 8192
DEPTH = 1

D_MIX = D_MODEL
D_SGU = D_MIX // 2
D_CONV = D_MIX - D_SGU
HEAD_DIM = 64
N_SGU_HEADS = D_SGU // HEAD_DIM
N_CONV_GROUPS = D_CONV // HEAD_DIM
CHUNK = 128
CONV_WIDTH = 31
D_FF = -(-8 * D_MODEL // (3 * 256)) * 256
ALPHA = (2.0 * DEPTH) ** 0.25
BETA = (8.0 * DEPTH) ** -0.25
LN_EPS = 1e-5

kernel_name = "hybrid_sgu_conformer_deepnorm"


def layer_norm(x, g, b):
    xf = x.astype(jnp.float32)
    mu = jnp.mean(xf, axis=-1, keepdims=True)
    var = jnp.mean(jnp.square(xf - mu), axis=-1, keepdims=True)
    y = (xf - mu) * lax.rsqrt(var + LN_EPS)
    return (y * g.astype(jnp.float32) + b.astype(jnp.float32)).astype(x.dtype)


def spatial_gating(z, ln_g, ln_b, w_s, b_s):
    u, v = jnp.split(z, 2, axis=-1)
    v = layer_norm(v, ln_g, ln_b)
    bsz, seq, _ = v.shape
    n_chunks = seq // CHUNK
    v = v.reshape(bsz, n_chunks, CHUNK, N_SGU_HEADS, HEAD_DIM)
    causal = jnp.tril(jnp.ones((CHUNK, CHUNK), dtype=bool))
    w = jnp.where(causal, w_s, 0).astype(v.dtype)
    mixed = jnp.einsum('hts,bcshd->bcthd', w, v) + b_s.T[None, None, :, :, None]
    return u * mixed.reshape(bsz, seq, D_SGU)


def conv_module(a, g, conv_w, conv_b, ln_g, ln_b):
    h = a * jax.nn.sigmoid(g)
    y = lax.conv_general_dilated(
        h, conv_w[:, None, :].astype(h.dtype),
        window_strides=(1,), padding=[(CONV_WIDTH - 1, 0)],
        dimension_numbers=('NWC', 'WIO', 'NWC'),
        feature_group_count=D_CONV) + conv_b
    y = layer_norm(y, ln_g, ln_b)
    return jax.nn.silu(y)


def hybrid_layer(x, w_in, sgu_ln_g, sgu_ln_b, w_s, b_s, conv_w, conv_b, conv_ln_g, conv_ln_b,
                 w_out, ln1_g, ln1_b, w_gate, w_up, w_down, ln2_g, ln2_b):
    proj = jnp.einsum('bsd,de->bse', x, w_in)
    z_sgu = jax.nn.gelu(proj[..., :2 * D_SGU], approximate=False)
    a_conv, g_conv = jnp.split(proj[..., 2 * D_SGU:], 2, axis=-1)
    y = jnp.concatenate([
        spatial_gating(z_sgu, sgu_ln_g, sgu_ln_b, w_s, b_s),
        conv_module(a_conv, g_conv, conv_w, conv_b, conv_ln_g, conv_ln_b),
    ], axis=-1)
    x = layer_norm(ALPHA * x + jnp.einsum('bse,ed->bsd', y, w_out), ln1_g, ln1_b)
    h = jax.nn.silu(jnp.einsum('bsd,df->bsf', x, w_gate)) * jnp.einsum('bsd,df->bsf', x, w_up)
    x = layer_norm(ALPHA * x + jnp.einsum('bsf,fd->bsd', h, w_down), ln2_g, ln2_b)
    return x


def setup_inputs(seed: int = 0) -> dict:
    key = jax.random.key(seed)
    ks = jax.random.split(key, 20)
    f32 = jnp.float32

    def nrm(k, shape, scale):
        return jax.random.normal(k, shape, f32) * scale

    L = DEPTH
    return {
        "x": nrm(ks[0], (BATCH, SEQ, D_MODEL), 1.0),
        "w_in": nrm(ks[1], (L, D_MODEL, 2 * D_SGU + 2 * D_CONV), D_MODEL ** -0.5),
        "sgu_ln_g": 1.0 + nrm(ks[2], (L, D_SGU), 0.02),
        "sgu_ln_b": nrm(ks[3], (L, D_SGU), 0.02),
        "w_s": nrm(ks[4], (L, N_SGU_HEADS, CHUNK, CHUNK), CHUNK ** -0.5),
        "b_s": 1.0 + nrm(ks[5], (L, N_SGU_HEADS, CHUNK), 0.02),
        "conv_w": nrm(ks[6], (L, CONV_WIDTH, D_CONV), CONV_WIDTH ** -0.5),
        "conv_b": nrm(ks[7], (L, D_CONV), 0.02),
        "conv_ln_g": 1.0 + nrm(ks[8], (L, D_CONV), 0.02),
        "conv_ln_b": nrm(ks[9], (L, D_CONV), 0.02),
        "w_out": nrm(ks[10], (L, D_MIX, D_MODEL), BETA * D_MIX ** -0.5),
        "ln1_g": 1.0 + nrm(ks[11], (L, D_MODEL), 0.02),
        "ln1_b": nrm(ks[12], (L, D_MODEL), 0.02),
        "w_gate": nrm(ks[13], (L, D_MODEL, D_FF), D_MODEL ** -0.5),
        "w_up": nrm(ks[14], (L, D_MODEL, D_FF), D_MODEL ** -0.5),
        "w_down": nrm(ks[15], (L, D_FF, D_MODEL), BETA * D_FF ** -0.5),
        "ln2_g": 1.0 + nrm(ks[16], (L, D_MODEL), 0.02),
        "ln2_b": nrm(ks[17], (L, D_MODEL), 0.02),
    }


def reference(x, w_in, sgu_ln_g, sgu_ln_b, w_s, b_s, conv_w, conv_b, conv_ln_g, conv_ln_b,
              w_out, ln1_g, ln1_b, w_gate, w_up, w_down, ln2_g, ln2_b):
    for l in range(DEPTH):
        x = hybrid_layer(x, w_in[l], sgu_ln_g[l], sgu_ln_b[l], w_s[l], b_s[l], conv_w[l], conv_b[l],
                         conv_ln_g[l], conv_ln_b[l], w_out[l], ln1_g[l], ln1_b[l],
                         w_gate[l], w_up[l], w_down[l], ln2_g[l], ln2_b[l])
    return x
```

```python
import functools

import jax
import jax.numpy as jnp
from jax import lax
from jax.experimental import pallas as pl
from jax.experimental.pallas import tpu as pltpu

HEAD_DIM = 64
CHUNK = 128
LN_EPS = 1e-5
LANES = 128
HALO = 32
VMEM_LIMIT_BYTES = 56 * 1024 * 1024


def _layer_norm(x, g, b):
    mu = jnp.mean(x, axis=-1, keepdims=True)
    xc = x - mu
    var = jnp.mean(xc * xc, axis=-1, keepdims=True)
    return xc * lax.rsqrt(var + LN_EPS) * g + b


def _gelu(x):
    return 0.5 * x * (1.0 + lax.erf(x * (0.5 ** 0.5)))


def _dot(a, b):
    return jnp.dot(a, b, preferred_element_type=jnp.float32)


def _mixer_kernel(x_ref, win_ref, sg_ref, sb_ref, ws_ref, bs_ref, cw_ref, cb_ref, cg_ref, cbb_ref,
                  wout_ref, g1_ref, b1_ref, o_ref, hext_ref, ycat_ref,
                  *, tiles_per_seq, alpha, conv_width):
    tm = x_ref.shape[0]
    d_sgu = sg_ref.shape[1]
    d_conv = cg_ref.shape[1]
    n_pairs = d_sgu // LANES
    n_chunks = tm // CHUNK

    x = x_ref[...]
    xb = x.astype(jnp.bfloat16)

    u = _gelu(_dot(xb, win_ref[:, 0:d_sgu]))
    v = _gelu(_dot(xb, win_ref[:, d_sgu:2 * d_sgu]))
    v = _layer_norm(v, sg_ref[...], sb_ref[...]).astype(jnp.bfloat16)

    row = lax.broadcasted_iota(jnp.int32, (CHUNK, 2 * CHUNK), 0)
    col = lax.broadcasted_iota(jnp.int32, (CHUNK, 2 * CHUNK), 1)
    causal = (col % CHUNK) <= row
    lane = lax.broadcasted_iota(jnp.int32, (CHUNK, LANES), 1)
    low_head = lane < HEAD_DIM
    zero = jnp.zeros((CHUNK, LANES), jnp.bfloat16)
    for j in range(n_pairs):
        w_pair = jnp.where(causal, ws_ref[j], jnp.zeros_like(ws_ref[j]))
        bias = bs_ref[:, j * LANES:(j + 1) * LANES]
        for c0 in range(0, n_chunks, 2):
            cs = list(range(c0, min(c0 + 2, n_chunks)))
            rhs = []
            for c in cs:
                vj = v[c * CHUNK:(c + 1) * CHUNK, j * LANES:(j + 1) * LANES]
                rhs.append(jnp.concatenate(
                    [jnp.where(low_head, vj, zero), jnp.where(low_head, zero, vj)], axis=0))
            mixed = _dot(w_pair, jnp.concatenate(rhs, axis=1) if len(rhs) > 1 else rhs[0])
            for i, c in enumerate(cs):
                uc = u[c * CHUNK:(c + 1) * CHUNK, j * LANES:(j + 1) * LANES]
                out_a = uc * (mixed[:, i * LANES:(i + 1) * LANES] + bias)
                ycat_ref[c * CHUNK:(c + 1) * CHUNK, j * LANES:(j + 1) * LANES] = out_a.astype(jnp.bfloat16)

    a = _dot(xb, win_ref[:, 2 * d_sgu:2 * d_sgu + d_conv])
    g = _dot(xb, win_ref[:, 2 * d_sgu + d_conv:2 * d_sgu + 2 * d_conv])
    h = a * jax.nn.sigmoid(g)

    first = (pl.program_id(0) % tiles_per_seq) == 0

    @pl.when(first)
    def _():
        hext_ref[0:HALO, :] = jnp.zeros((HALO, d_conv), jnp.float32)

    @pl.when(jnp.logical_not(first))
    def _():
        hext_ref[0:HALO, :] = hext_ref[tm:tm + HALO, :]

    hext_ref[HALO:HALO + tm, :] = h

    base = HALO - (conv_width - 1)
    y = jnp.zeros((tm, d_conv), jnp.float32)
    for k in range(conv_width):
        y = y + hext_ref[base + k:base + k + tm, :] * cw_ref[k:k + 1, :]
    y = y + cb_ref[...]
    y = _layer_norm(y, cg_ref[...], cbb_ref[...])
    y = y * jax.nn.sigmoid(y)
    ycat_ref[:, d_sgu:d_sgu + d_conv] = y.astype(jnp.bfloat16)

    r = alpha * x + _dot(ycat_ref[...], wout_ref[...])
    o_ref[...] = _layer_norm(r, g1_ref[...], b1_ref[...])


def _ffn_kernel(x_ref, wg_ref, wu_ref, wd_ref, g2_ref, b2_ref, o_ref, h_ref, *, alpha, ff_chunk):
    x = x_ref[...]
    xb = x.astype(jnp.bfloat16)
    d_ff = wg_ref.shape[1]
    for c in range(0, d_ff, ff_chunk):
        gate = _dot(xb, wg_ref[:, c:c + ff_chunk])
        up = _dot(xb, wu_ref[:, c:c + ff_chunk])
        h_ref[:, c:c + ff_chunk] = (gate * jax.nn.sigmoid(gate) * up).astype(jnp.bfloat16)
    r = alpha * x + _dot(h_ref[...], wd_ref[...])
    o_ref[...] = _layer_norm(r, g2_ref[...], b2_ref[...])


def _const_spec(shape):
    return pl.BlockSpec(shape, lambda i: (0,) * len(shape), pipeline_mode=pl.Buffered(1))


def _layer(x2d, seq, w_in, sgu_ln_g, sgu_ln_b, w_s, b_s, conv_w, conv_b, conv_ln_g, conv_ln_b,
           w_out, ln1_g, ln1_b, w_gate, w_up, w_down, ln2_g, ln2_b, alpha):
    n_tok, d_model = x2d.shape
    d_sgu = sgu_ln_g.shape[0]
    d_conv = conv_ln_g.shape[0]
    n_heads, chunk, _ = w_s.shape
    conv_width = conv_w.shape[0]
    d_ff = w_gate.shape[1]
    assert chunk == CHUNK and n_heads * HEAD_DIM == d_sgu and d_sgu % LANES == 0
    assert conv_width - 1 <= HALO

    tm = 512
    assert seq % tm == 0 and n_tok % tm == 0 and tm % CHUNK == 0
    n_tiles = n_tok // tm

    bf16 = jnp.bfloat16
    row = lambda p: p.reshape(1, -1)
    ws_pairs = w_s.reshape(n_heads // 2, 2, chunk, chunk).transpose(0, 2, 1, 3).reshape(
        n_heads // 2, chunk, 2 * chunk).astype(bf16)
    bias = jnp.repeat(b_s.T, HEAD_DIM, axis=1)

    tile = lambda width: pl.BlockSpec((tm, width), lambda i: (i, 0))
    params = pltpu.CompilerParams(dimension_semantics=("arbitrary",), vmem_limit_bytes=VMEM_LIMIT_BYTES)

    x1 = pl.pallas_call(
        functools.partial(_mixer_kernel, tiles_per_seq=seq // tm, alpha=alpha, conv_width=conv_width),
        grid=(n_tiles,),
        in_specs=[
            tile(d_model),
            _const_spec((d_model, 2 * d_sgu + 2 * d_conv)),
            _const_spec((1, d_sgu)), _const_spec((1, d_sgu)),
            _const_spec((n_heads // 2, chunk, 2 * chunk)),
            _const_spec((chunk, d_sgu)),
            _const_spec((conv_width, d_conv)),
            _const_spec((1, d_conv)), _const_spec((1, d_conv)), _const_spec((1, d_conv)),
            _const_spec((d_sgu + d_conv, d_model)),
            _const_spec((1, d_model)), _const_spec((1, d_model)),
        ],
        out_specs=tile(d_model),
        out_shape=jax.ShapeDtypeStruct((n_tok, d_model), jnp.float32),
        scratch_shapes=[
            pltpu.VMEM((HALO + tm, d_conv), jnp.float32),
            pltpu.VMEM((tm, d_sgu + d_conv), bf16),
        ],
        compiler_params=params,
        name="mixer",
    )(x2d, w_in.astype(bf16), row(sgu_ln_g), row(sgu_ln_b), ws_pairs, bias, conv_w, row(conv_b),
      row(conv_ln_g), row(conv_ln_b), w_out.astype(bf16), row(ln1_g), row(ln1_b))

    return pl.pallas_call(
        functools.partial(_ffn_kernel, alpha=alpha, ff_chunk=256),
        grid=(n_tiles,),
        in_specs=[
            tile(d_model),
            _const_spec((d_model, d_ff)), _const_spec((d_model, d_ff)), _const_spec((d_ff, d_model)),
            _const_spec((1, d_model)), _const_spec((1, d_model)),
        ],
        out_specs=tile(d_model),
        out_shape=jax.ShapeDtypeStruct((n_tok, d_model), jnp.float32),
        scratch_shapes=[pltpu.VMEM((tm, d_ff), bf16)],
        compiler_params=params,
        name="ffn",
    )(x1, w_gate.astype(bf16), w_up.astype(bf16), w_down.astype(bf16), row(ln2_g), row(ln2_b))


def kernel(x, w_in, sgu_ln_g, sgu_ln_b, w_s, b_s, conv_w, conv_b, conv_ln_g, conv_ln_b, w_out, ln1_g, ln1_b, w_gate, w_up, w_down, ln2_g, ln2_b):
    batch, seq, d_model = x.shape
    depth = w_in.shape[0]
    alpha = (2.0 * depth) ** 0.25
    x2d = x.reshape(batch * seq, d_model)
    for l in range(depth):
        x2d = _layer(x2d, seq, w_in[l], sgu_ln_g[l], sgu_ln_b[l], w_s[l], b_s[l], conv_w[l], conv_b[l],
                     conv_ln_g[l], conv_ln_b[l], w_out[l], ln1_g[l], ln1_b[l],
                     w_gate[l], w_up[l], w_down[l], ln2_g[l], ln2_b[l], alpha)
    return x2d.reshape(batch, seq, d_model)
```

```python
import functools

import jax
import jax.numpy as jnp
from jax import lax
from jax.experimental import pallas as pl
from jax.experimental.pallas import tpu as pltpu

HEAD_DIM = 64
CHUNK = 128
LN_EPS = 1e-5
LANES = 128
SUBLANES = 8
HALO = 32
CONV_ROWS = 64
VMEM_LIMIT_BYTES = 56 * 1024 * 1024


def _layer_norm(x, g, b):
    mu = jnp.mean(x, axis=-1, keepdims=True)
    xc = x - mu
    var = jnp.mean(xc * xc, axis=-1, keepdims=True)
    return xc * lax.rsqrt(var + LN_EPS) * g + b


def _gelu(x):
    return 0.5 * x * (1.0 + lax.erf(x * (0.5 ** 0.5)))


def _dot(a, b):
    return jnp.dot(a, b, preferred_element_type=jnp.float32)


def _mixer_kernel(x_ref, win_ref, sg_ref, sb_ref, ws_ref, bs_ref, cw_ref, cb_ref, cg_ref, cbb_ref,
                  wout_ref, g1_ref, b1_ref, o_ref, hext_ref, yconv_ref, ycat_ref,
                  *, tiles_per_seq, alpha, conv_width):
    tm = x_ref.shape[0]
    d_sgu = sg_ref.shape[1]
    d_conv = cg_ref.shape[1]
    n_pairs = d_sgu // LANES
    n_chunks = tm // CHUNK

    @pl.when(pl.program_id(0) == 0)
    def _():
        hext_ref[tm:tm + HALO, :] = jnp.zeros((HALO, d_conv), jnp.float32)

    x = x_ref[...]
    xb = x.astype(jnp.bfloat16)

    u = _gelu(_dot(xb, win_ref[:, 0:d_sgu]))
    v = _gelu(_dot(xb, win_ref[:, d_sgu:2 * d_sgu]))
    v = _layer_norm(v, sg_ref[...], sb_ref[...]).astype(jnp.bfloat16)

    row = lax.broadcasted_iota(jnp.int32, (CHUNK, 2 * CHUNK), 0)
    col = lax.broadcasted_iota(jnp.int32, (CHUNK, 2 * CHUNK), 1)
    causal = (col % CHUNK) <= row
    lane = lax.broadcasted_iota(jnp.int32, (CHUNK, LANES), 1)
    low_head = lane < HEAD_DIM
    zero = jnp.zeros((CHUNK, LANES), jnp.bfloat16)
    for j in range(n_pairs):
        w_pair = jnp.where(causal, ws_ref[j], jnp.zeros_like(ws_ref[j]))
        bias = bs_ref[:, j * LANES:(j + 1) * LANES]
        for c0 in range(0, n_chunks, 2):
            cs = list(range(c0, min(c0 + 2, n_chunks)))
            rhs = []
            for c in cs:
                vj = v[c * CHUNK:(c + 1) * CHUNK, j * LANES:(j + 1) * LANES]
                rhs.append(jnp.concatenate(
                    [jnp.where(low_head, vj, zero), jnp.where(low_head, zero, vj)], axis=0))
            mixed = _dot(w_pair, jnp.concatenate(rhs, axis=1) if len(rhs) > 1 else rhs[0])
            for i, c in enumerate(cs):
                uc = u[c * CHUNK:(c + 1) * CHUNK, j * LANES:(j + 1) * LANES]
                out_a = uc * (mixed[:, i * LANES:(i + 1) * LANES] + bias)
                ycat_ref[c * CHUNK:(c + 1) * CHUNK, j * LANES:(j + 1) * LANES] = out_a.astype(jnp.bfloat16)

    a = _dot(xb, win_ref[:, 2 * d_sgu:2 * d_sgu + d_conv])
    g = _dot(xb, win_ref[:, 2 * d_sgu + d_conv:2 * d_sgu + 2 * d_conv])
    h = a * jax.nn.sigmoid(g)

    seq_start = (pl.program_id(0) % tiles_per_seq) == 0
    hext_ref[0:HALO, :] = jnp.where(seq_start, 0.0, hext_ref[tm:tm + HALO, :])
    hext_ref[HALO:HALO + tm, :] = h

    base = HALO - (conv_width - 1)
    span = CONV_ROWS + HALO
    for cb in range(d_conv // LANES):
        lanes = slice(cb * LANES, (cb + 1) * LANES)
        for t0 in range(0, tm, CONV_ROWS):
            src = hext_ref[t0:t0 + span, lanes]
            acc = None
            for r in range(SUBLANES):
                taps = [o for o in range(base, base + conv_width) if o % SUBLANES == r]
                rot = src if r == 0 else pltpu.roll(src, span - r, 0)
                for o in taps:
                    term = rot[o - r:o - r + CONV_ROWS, :] * cw_ref[o - base:o - base + 1, lanes]
                    acc = term if acc is None else acc + term
            yconv_ref[t0:t0 + CONV_ROWS, lanes] = acc
    y = yconv_ref[...] + cb_ref[...]
    y = _layer_norm(y, cg_ref[...], cbb_ref[...])
    y = y * jax.nn.sigmoid(y)
    ycat_ref[:, d_sgu:d_sgu + d_conv] = y.astype(jnp.bfloat16)

    r = alpha * x + _dot(ycat_ref[...], wout_ref[...])
    o_ref[...] = _layer_norm(r, g1_ref[...], b1_ref[...])


def _ffn_kernel(x_ref, wg_ref, wu_ref, wd_ref, g2_ref, b2_ref, o_ref, h_ref, *, alpha, ff_chunk):
    x = x_ref[...]
    xb = x.astype(jnp.bfloat16)
    d_ff = wg_ref.shape[1]
    for c in range(0, d_ff, ff_chunk):
        gate = _dot(xb, wg_ref[:, c:c + ff_chunk])
        up = _dot(xb, wu_ref[:, c:c + ff_chunk])
        h_ref[:, c:c + ff_chunk] = (gate * jax.nn.sigmoid(gate) * up).astype(jnp.bfloat16)
    r = alpha * x + _dot(h_ref[...], wd_ref[...])
    o_ref[...] = _layer_norm(r, g2_ref[...], b2_ref[...])


def _const_spec(shape):
    return pl.BlockSpec(shape, lambda i: (0,) * len(shape), pipeline_mode=pl.Buffered(1))


def _layer(x2d, seq, w_in, sgu_ln_g, sgu_ln_b, w_s, b_s, conv_w, conv_b, conv_ln_g, conv_ln_b,
           w_out, ln1_g, ln1_b, w_gate, w_up, w_down, ln2_g, ln2_b, alpha):
    n_tok, d_model = x2d.shape
    d_sgu = sgu_ln_g.shape[0]
    d_conv = conv_ln_g.shape[0]
    n_heads, chunk, _ = w_s.shape
    conv_width = conv_w.shape[0]
    d_ff = w_gate.shape[1]
    assert chunk == CHUNK and n_heads * HEAD_DIM == d_sgu and d_sgu % LANES == 0
    assert conv_width - 1 <= HALO

    tm = 512
    assert seq % tm == 0 and n_tok % tm == 0 and tm % CHUNK == 0
    n_tiles = n_tok // tm

    bf16 = jnp.bfloat16
    row = lambda p: p.reshape(1, -1)
    ws_pairs = w_s.reshape(n_heads // 2, 2, chunk, chunk).transpose(0, 2, 1, 3).reshape(
        n_heads // 2, chunk, 2 * chunk).astype(bf16)
    bias = jnp.repeat(b_s.T, HEAD_DIM, axis=1)

    tile = lambda width: pl.BlockSpec((tm, width), lambda i: (i, 0))
    params = pltpu.CompilerParams(dimension_semantics=("arbitrary",), vmem_limit_bytes=VMEM_LIMIT_BYTES)

    x1 = pl.pallas_call(
        functools.partial(_mixer_kernel, tiles_per_seq=seq // tm, alpha=alpha, conv_width=conv_width),
        grid=(n_tiles,),
        in_specs=[
            tile(d_model),
            _const_spec((d_model, 2 * d_sgu + 2 * d_conv)),
            _const_spec((1, d_sgu)), _const_spec((1, d_sgu)),
            _const_spec((n_heads // 2, chunk, 2 * chunk)),
            _const_spec((chunk, d_sgu)),
            _const_spec((conv_width, d_conv)),
            _const_spec((1, d_conv)), _const_spec((1, d_conv)), _const_spec((1, d_conv)),
            _const_spec((d_sgu + d_conv, d_model)),
            _const_spec((1, d_model)), _const_spec((1, d_model)),
        ],
        out_specs=tile(d_model),
        out_shape=jax.ShapeDtypeStruct((n_tok, d_model), jnp.float32),
        scratch_shapes=[
            pltpu.VMEM((HALO + tm, d_conv), jnp.float32),
            pltpu.VMEM((tm, d_conv), jnp.float32),
            pltpu.VMEM((tm, d_sgu + d_conv), bf16),
        ],
        compiler_params=params,
        name="mixer",
    )(x2d, w_in.astype(bf16), row(sgu_ln_g), row(sgu_ln_b), ws_pairs, bias, conv_w, row(conv_b),
      row(conv_ln_g), row(conv_ln_b), w_out.astype(bf16), row(ln1_g), row(ln1_b))

    return pl.pallas_call(
        functools.partial(_ffn_kernel, alpha=alpha, ff_chunk=256),
        grid=(n_tiles,),
        in_specs=[
            tile(d_model),
            _const_spec((d_model, d_ff)), _const_spec((d_model, d_ff)), _const_spec((d_ff, d_model)),
            _const_spec((1, d_model)), _const_spec((1, d_model)),
        ],
        out_specs=tile(d_model),
        out_shape=jax.ShapeDtypeStruct((n_tok, d_model), jnp.float32),
        scratch_shapes=[pltpu.VMEM((tm, d_ff), bf16)],
        compiler_params=params,
        name="ffn",
    )(x1, w_gate.astype(bf16), w_up.astype(bf16), w_down.astype(bf16), row(ln2_g), row(ln2_b))


def kernel(x, w_in, sgu_ln_g, sgu_ln_b, w_s, b_s, conv_w, conv_b, conv_ln_g, conv_ln_b, w_out, ln1_g, ln1_b, w_gate, w_up, w_down, ln2_g, ln2_b):
    batch, seq, d_model = x.shape
    depth = w_in.shape[0]
    alpha = (2.0 * depth) ** 0.25
    x2d = x.reshape(batch * seq, d_model)
    for l in range(depth):
        x2d = _layer(x2d, seq, w_in[l], sgu_ln_g[l], sgu_ln_b[l], w_s[l], b_s[l], conv_w[l], conv_b[l],
                     conv_ln_g[l], conv_ln_b[l], w_out[l], ln1_g[l], ln1_b[l],
                     w_gate[l], w_up[l], w_down[l], ln2_g[l], ln2_b[l], alpha)
    return x2d.reshape(batch, seq, d_model)
```

```python
import functools

import jax
import jax.numpy as jnp
from jax import lax
from jax.experimental import pallas as pl
from jax.experimental.pallas import tpu as pltpu

HEAD_DIM = 64
CHUNK = 128
LN_EPS = 1e-5
LANES = 128
SUBLANES = 8
HALO = 32
CONV_ROWS = 64
FF_CHUNK = 256
TM_MIXER = 512
TM_FFN = 1024
FFN_ROWS = 512
VMEM_LIMIT_BYTES = 56 * 1024 * 1024


def _layer_norm(x, g, b):
    mu = jnp.mean(x, axis=-1, keepdims=True)
    xc = x - mu
    var = jnp.mean(xc * xc, axis=-1, keepdims=True)
    return xc * lax.rsqrt(var + LN_EPS) * g + b


def _gelu(x):
    return 0.5 * x * (1.0 + lax.erf(x * (0.5 ** 0.5)))


def _dot(a, b):
    return jnp.dot(a, b, preferred_element_type=jnp.float32)


def _mixer_kernel(x_ref, win_ref, sg_ref, sb_ref, ws_ref, bs_ref, cw_ref, cb_ref, cg_ref, cbb_ref,
                  wout_ref, g1_ref, b1_ref, o_ref, hext_ref, yconv_ref, ycat_ref,
                  *, tiles_per_seq, alpha):
    tm = x_ref.shape[0]
    d_sgu = sg_ref.shape[1]
    conv_width, d_conv = cw_ref.shape
    n_pairs = d_sgu // LANES
    n_chunks = tm // CHUNK

    @pl.when(pl.program_id(0) == 0)
    def _():
        hext_ref[tm:tm + HALO, :] = jnp.zeros((HALO, d_conv), jnp.float32)

    x = x_ref[...]
    xb = x.astype(jnp.bfloat16)

    u = _gelu(_dot(xb, win_ref[:, 0:d_sgu]))
    v = _gelu(_dot(xb, win_ref[:, d_sgu:2 * d_sgu]))
    v = _layer_norm(v, sg_ref[...], sb_ref[...]).astype(jnp.bfloat16)

    row = lax.broadcasted_iota(jnp.int32, (CHUNK, 2 * CHUNK), 0)
    col = lax.broadcasted_iota(jnp.int32, (CHUNK, 2 * CHUNK), 1)
    causal = (col % CHUNK) <= row
    lane = lax.broadcasted_iota(jnp.int32, (CHUNK, LANES), 1)
    low_head = lane < HEAD_DIM
    zero = jnp.zeros((CHUNK, LANES), jnp.bfloat16)
    for j in range(n_pairs):
        w_pair = jnp.where(causal, ws_ref[j], jnp.zeros_like(ws_ref[j]))
        bias = bs_ref[:, j * LANES:(j + 1) * LANES]
        for c0 in range(0, n_chunks, 2):
            cs = list(range(c0, min(c0 + 2, n_chunks)))
            rhs = []
            for c in cs:
                vj = v[c * CHUNK:(c + 1) * CHUNK, j * LANES:(j + 1) * LANES]
                rhs.append(jnp.concatenate(
                    [jnp.where(low_head, vj, zero), jnp.where(low_head, zero, vj)], axis=0))
            mixed = _dot(w_pair, jnp.concatenate(rhs, axis=1) if len(rhs) > 1 else rhs[0])
            for i, c in enumerate(cs):
                uc = u[c * CHUNK:(c + 1) * CHUNK, j * LANES:(j + 1) * LANES]
                out_a = uc * (mixed[:, i * LANES:(i + 1) * LANES] + bias)
                ycat_ref[c * CHUNK:(c + 1) * CHUNK, j * LANES:(j + 1) * LANES] = out_a.astype(jnp.bfloat16)

    a = _dot(xb, win_ref[:, 2 * d_sgu:2 * d_sgu + d_conv])
    g = _dot(xb, win_ref[:, 2 * d_sgu + d_conv:2 * d_sgu + 2 * d_conv])
    h = a * jax.nn.sigmoid(g)

    seq_start = (pl.program_id(0) % tiles_per_seq) == 0
    hext_ref[0:HALO, :] = jnp.where(seq_start, 0.0, hext_ref[tm:tm + HALO, :])
    hext_ref[HALO:HALO + tm, :] = h

    base = HALO - (conv_width - 1)
    span = CONV_ROWS + HALO
    for cb in range(d_conv // LANES):
        lanes = slice(cb * LANES, (cb + 1) * LANES)
        for t0 in range(0, tm, CONV_ROWS):
            src = hext_ref[t0:t0 + span, lanes]
            acc = None
            for r in range(SUBLANES):
                taps = [o for o in range(base, base + conv_width) if o % SUBLANES == r]
                rot = src if r == 0 else pltpu.roll(src, span - r, 0)
                for o in taps:
                    term = rot[o - r:o - r + CONV_ROWS, :] * cw_ref[o - base:o - base + 1, lanes]
                    acc = term if acc is None else acc + term
            yconv_ref[t0:t0 + CONV_ROWS, lanes] = acc
    y = yconv_ref[...] + cb_ref[...]
    y = _layer_norm(y, cg_ref[...], cbb_ref[...])
    y = y * jax.nn.sigmoid(y)
    ycat_ref[:, d_sgu:d_sgu + d_conv] = y.astype(jnp.bfloat16)

    r = alpha * x + _dot(ycat_ref[...], wout_ref[...])
    o_ref[...] = _layer_norm(r, g1_ref[...], b1_ref[...])


def _ffn_kernel(x_ref, wg_ref, wu_ref, wd_ref, g2_ref, b2_ref, o_ref, h_ref, *, alpha):
    d_ff = wg_ref.shape[1]
    for r0 in range(0, x_ref.shape[0], FFN_ROWS):
        rows = slice(r0, r0 + FFN_ROWS)
        x = x_ref[rows, :]
        xb = x.astype(jnp.bfloat16)
        for c in range(0, d_ff, FF_CHUNK):
            gate = _dot(xb, wg_ref[:, c:c + FF_CHUNK])
            up = _dot(xb, wu_ref[:, c:c + FF_CHUNK])
            h_ref[rows, c:c + FF_CHUNK] = (gate * jax.nn.sigmoid(gate) * up).astype(jnp.bfloat16)
        r = alpha * x + _dot(h_ref[rows, :], wd_ref[...])
        o_ref[rows, :] = _layer_norm(r, g2_ref[...], b2_ref[...])


def _const_spec(shape):
    return pl.BlockSpec(shape, lambda i: (0,) * len(shape), pipeline_mode=pl.Buffered(1))


def _tile_spec(tm, width):
    return pl.BlockSpec((tm, width), lambda i: (i, 0))


def _layer(x2d, seq, w_in, sgu_ln_g, sgu_ln_b, w_s, b_s, conv_w, conv_b, conv_ln_g, conv_ln_b,
           w_out, ln1_g, ln1_b, w_gate, w_up, w_down, ln2_g, ln2_b, alpha):
    n_tok, d_model = x2d.shape
    d_sgu = sgu_ln_g.shape[0]
    d_conv = conv_ln_g.shape[0]
    n_heads, chunk, _ = w_s.shape
    conv_width = conv_w.shape[0]
    d_ff = w_gate.shape[1]
    assert chunk == CHUNK and n_heads * HEAD_DIM == d_sgu and d_sgu % LANES == 0
    assert conv_width - 1 <= HALO and d_ff % FF_CHUNK == 0
    assert seq % TM_MIXER == 0 and TM_MIXER % CHUNK == 0 and TM_MIXER % CONV_ROWS == 0
    assert n_tok % TM_FFN == 0

    bf16 = jnp.bfloat16
    row = lambda p: p.reshape(1, -1)
    ws_pairs = w_s.reshape(n_heads // 2, 2, chunk, chunk).transpose(0, 2, 1, 3).reshape(
        n_heads // 2, chunk, 2 * chunk).astype(bf16)
    bias = jnp.repeat(b_s.T, HEAD_DIM, axis=1)

    params = pltpu.CompilerParams(dimension_semantics=("arbitrary",), vmem_limit_bytes=VMEM_LIMIT_BYTES)

    tm = TM_MIXER
    x1 = pl.pallas_call(
        functools.partial(_mixer_kernel, tiles_per_seq=seq // tm, alpha=alpha),
        grid=(n_tok // tm,),
        in_specs=[
            _tile_spec(tm, d_model),
            _const_spec((d_model, 2 * d_sgu + 2 * d_conv)),
            _const_spec((1, d_sgu)), _const_spec((1, d_sgu)),
            _const_spec((n_heads // 2, chunk, 2 * chunk)),
            _const_spec((chunk, d_sgu)),
            _const_spec((conv_width, d_conv)),
            _const_spec((1, d_conv)), _const_spec((1, d_conv)), _const_spec((1, d_conv)),
            _const_spec((d_sgu + d_conv, d_model)),
            _const_spec((1, d_model)), _const_spec((1, d_model)),
        ],
        out_specs=_tile_spec(tm, d_model),
        out_shape=jax.ShapeDtypeStruct((n_tok, d_model), jnp.float32),
        scratch_shapes=[
            pltpu.VMEM((HALO + tm, d_conv), jnp.float32),
            pltpu.VMEM((tm, d_conv), jnp.float32),
            pltpu.VMEM((tm, d_sgu + d_conv), bf16),
        ],
        compiler_params=params,
        name="mixer",
    )(x2d, w_in.astype(bf16), row(sgu_ln_g), row(sgu_ln_b), ws_pairs, bias, conv_w, row(conv_b),
      row(conv_ln_g), row(conv_ln_b), w_out.astype(bf16), row(ln1_g), row(ln1_b))

    tm = TM_FFN
    return pl.pallas_call(
        functools.partial(_ffn_kernel, alpha=alpha),
        grid=(n_tok // tm,),
        in_specs=[
            _tile_spec(tm, d_model),
            _const_spec((d_model, d_ff)), _const_spec((d_model, d_ff)), _const_spec((d_ff, d_model)),
            _const_spec((1, d_model)), _const_spec((1, d_model)),
        ],
        out_specs=_tile_spec(tm, d_model),
        out_shape=jax.ShapeDtypeStruct((n_tok, d_model), jnp.float32),
        scratch_shapes=[pltpu.VMEM((tm, d_ff), bf16)],
        compiler_params=params,
        name="ffn",
    )(x1, w_gate.astype(bf16), w_up.astype(bf16), w_down.astype(bf16), row(ln2_g), row(ln2_b))


def kernel(x, w_in, sgu_ln_g, sgu_ln_b, w_s, b_s, conv_w, conv_b, conv_ln_g, conv_ln_b, w_out, ln1_g, ln1_b, w_gate, w_up, w_down, ln2_g, ln2_b):
    batch, seq, d_model = x.shape
    depth = w_in.shape[0]
    alpha = (2.0 * depth) ** 0.25
    x2d = x.reshape(batch * seq, d_model)
    for l in range(depth):
        x2d = _layer(x2d, seq, w_in[l], sgu_ln_g[l], sgu_ln_b[l], w_s[l], b_s[l], conv_w[l], conv_b[l],
                     conv_ln_g[l], conv_ln_b[l], w_out[l], ln1_g[l], ln1_b[l],
                     w_gate[l], w_up[l], w_down[l], ln2_g[l], ln2_b[l], alpha)
    return x2d.reshape(batch, seq, d_model)
```

```python
import functools

import jax
import jax.numpy as jnp
from jax import lax
from jax.experimental import pallas as pl
from jax.experimental.pallas import tpu as pltpu

HEAD_DIM = 64
CHUNK = 128
LN_EPS = 1e-5
LANES = 128
SUBLANES = 8
MXU_COLS = 256
HALO = 32
CONV_ROWS = 64
ACC_ROWS = 16
FF_CHUNK = MXU_COLS
UNITS_PER_CHUNK = 3
LAG = 2
TM = 512
VMEM_LIMIT_BYTES = 56 * 1024 * 1024


def _layer_norm(x, g, b):
    mu = jnp.mean(x, axis=-1, keepdims=True)
    xc = x - mu
    var = jnp.mean(xc * xc, axis=-1, keepdims=True)
    return xc * lax.rsqrt(var + LN_EPS) * g + b


def _gelu(x):
    return 0.5 * x * (1.0 + lax.erf(x * (0.5 ** 0.5)))


def _dot(a, b):
    return jnp.dot(a, b, preferred_element_type=jnp.float32)


def _zero_bits(dep):
    bits = pltpu.bitcast(dep[0:SUBLANES, 0:LANES], jnp.uint32)
    return lax.shift_right_logical(lax.shift_right_logical(bits, jnp.uint32(16)), jnp.uint32(16))


def _after(x, dep):
    zero = jnp.tile(_zero_bits(dep), (x.shape[0] // SUBLANES, x.shape[1] // LANES))
    return jnp.where(zero == 0, x, 0.0)


def _front_kernel(x_ref, win_ref, sg_ref, sb_ref, ws_ref, bs_ref, h_ref, ya_ref):
    tm = x_ref.shape[0]
    d_sgu = sg_ref.shape[1]
    d_conv = h_ref.shape[1]
    n_chunks = tm // CHUNK
    xb = x_ref[...].astype(jnp.bfloat16)

    a = _dot(xb, win_ref[:, 2 * d_sgu:2 * d_sgu + d_conv])
    g = _dot(xb, win_ref[:, 2 * d_sgu + d_conv:2 * d_sgu + 2 * d_conv])
    h_ref[...] = a * jax.nn.sigmoid(g)

    u = _gelu(_dot(xb, win_ref[:, 0:d_sgu]))
    v = _gelu(_dot(xb, win_ref[:, d_sgu:2 * d_sgu]))
    v = _layer_norm(v, sg_ref[...], sb_ref[...]).astype(jnp.bfloat16)

    row = lax.broadcasted_iota(jnp.int32, (CHUNK, 2 * CHUNK), 0)
    col = lax.broadcasted_iota(jnp.int32, (CHUNK, 2 * CHUNK), 1)
    causal = (col % CHUNK) <= row
    lane = lax.broadcasted_iota(jnp.int32, (CHUNK, LANES), 1)
    low_head = lane < HEAD_DIM
    zero = jnp.zeros((CHUNK, LANES), jnp.bfloat16)
    for j in range(d_sgu // LANES):
        w_pair = jnp.where(causal, ws_ref[j], jnp.zeros_like(ws_ref[j]))
        bias = bs_ref[:, j * LANES:(j + 1) * LANES]
        for c0 in range(0, n_chunks, 2):
            cs = list(range(c0, min(c0 + 2, n_chunks)))
            rhs = []
            for c in cs:
                vj = v[c * CHUNK:(c + 1) * CHUNK, j * LANES:(j + 1) * LANES]
                rhs.append(jnp.concatenate(
                    [jnp.where(low_head, vj, zero), jnp.where(low_head, zero, vj)], axis=0))
            mixed = _dot(w_pair, jnp.concatenate(rhs, axis=1) if len(rhs) > 1 else rhs[0])
            for i, c in enumerate(cs):
                uc = u[c * CHUNK:(c + 1) * CHUNK, j * LANES:(j + 1) * LANES]
                out_a = uc * (mixed[:, i * LANES:(i + 1) * LANES] + bias)
                ya_ref[c * CHUNK:(c + 1) * CHUNK, j * LANES:(j + 1) * LANES] = out_a.astype(jnp.bfloat16)


def _conv_unit(unit, hext_ref, cw_ref, yconv_ref, rot_ref, after):
    conv_width, d_conv = cw_ref.shape
    n_lane = d_conv // LANES
    t0 = (unit // n_lane) * CONV_ROWS
    lanes = slice((unit % n_lane) * LANES, (unit % n_lane + 1) * LANES)
    base = HALO - (conv_width - 1)
    span = CONV_ROWS + HALO
    src = _after(hext_ref[t0:t0 + span, lanes], after)
    rot_ref[0] = src
    for r in range(1, SUBLANES):
        rot_ref[r] = pltpu.roll(src, span - r, 0)
    acc = None
    for i0 in range(0, CONV_ROWS, ACC_ROWS):
        prev, acc = acc, None
        for o in range(base, base + conv_width):
            r = o % SUBLANES
            term = rot_ref[r, o - r + i0:o - r + i0 + ACC_ROWS, :] * cw_ref[o - base:o - base + 1, lanes]
            if acc is None:
                acc = term if prev is None else _after(term, prev)
            else:
                acc = acc + term
        yconv_ref[t0 + i0:t0 + i0 + ACC_ROWS, lanes] = acc
    return acc


def _gated_lhs(lhs, head_f32, token):
    head = _after(head_f32, token).astype(jnp.bfloat16)
    return jnp.concatenate([head, lhs[:, MXU_COLS:]], axis=1)


def _back_kernel(h_ref, ya_ref, x_ref, cw_ref, cb_ref, cg_ref, cbb_ref, wout_ref, g1_ref, b1_ref,
                 wg_ref, wu_ref, wd_ref, g2_ref, b2_ref, o_ref,
                 x1_ref, hext_ref, yconv_ref, rot_ref, ycat_ref, hid_ref, res_ref, *, tiles_per_seq, alpha):
    tm, d_model = x_ref.shape
    d_conv = h_ref.shape[1]
    d_sgu = ya_ref.shape[1]
    d_ff = wg_ref.shape[1]
    n_chunks = d_ff // FF_CHUNK
    n_units = (tm // CONV_ROWS) * (d_conv // LANES)
    step = pl.program_id(0)

    @pl.when(step == 0)
    def _():
        x1_ref[...] = jnp.zeros_like(x1_ref)
        hext_ref[tm:tm + HALO, :] = jnp.zeros((HALO, d_conv), jnp.float32)

    seq_start = (step % tiles_per_seq) == 0
    hext_ref[0:HALO, :] = jnp.where(seq_start, 0.0, hext_ref[tm:tm + HALO, :])
    hext_ref[HALO:HALO + tm, :] = h_ref[...]

    xfb = x1_ref[...].astype(jnp.bfloat16)
    res_ref[...] = alpha * x1_ref[...]
    unit = 0
    done = []
    for ci in range(n_chunks):
        cols = slice(ci * FF_CHUNK, (ci + 1) * FF_CHUNK)
        lhs = xfb if ci < LAG else _gated_lhs(xfb, x1_ref[:, 0:MXU_COLS], done[ci - LAG])
        gate = _dot(lhs, wg_ref[:, cols])
        up = _dot(lhs, wu_ref[:, cols])
        hid_ref[:, cols] = (gate * jax.nn.sigmoid(gate) * up).astype(jnp.bfloat16)
        token = gate
        for _ in range(UNITS_PER_CHUNK if ci < n_chunks - 1 else n_units - unit):
            token = _conv_unit(unit, hext_ref, cw_ref, yconv_ref, rot_ref, token)
            unit += 1
        done.append(token)

    def down_half(half, token):
        lhs = _gated_lhs(hid_ref[...], hid_ref[:, 0:MXU_COLS].astype(jnp.float32), token)
        for n in range(2 * half, 2 * half + 2):
            cols = slice(n * (d_model // 4), (n + 1) * (d_model // 4))
            res_ref[:, cols] += _dot(lhs, wd_ref[:, cols])

    down_half(0, done[n_chunks - LAG])

    y = yconv_ref[...] + cb_ref[...]
    y = _layer_norm(y, cg_ref[...], cbb_ref[...])
    y = y * jax.nn.sigmoid(y)
    ycat_ref[:, 0:d_sgu] = ya_ref[...]
    ycat_ref[:, d_sgu:d_sgu + d_conv] = y.astype(jnp.bfloat16)
    r1 = alpha * x_ref[...] + _dot(ycat_ref[...], wout_ref[...])
    x1_new = _layer_norm(r1, g1_ref[...], b1_ref[...])
    x1_ref[...] = x1_new

    down_half(1, r1)

    o_ref[...] = _layer_norm(res_ref[...], g2_ref[...], b2_ref[...])


def _const_spec(shape):
    return pl.BlockSpec(shape, lambda i: (0,) * len(shape), pipeline_mode=pl.Buffered(1))


def _layer(x2d, seq, w_in, sgu_ln_g, sgu_ln_b, w_s, b_s, conv_w, conv_b, conv_ln_g, conv_ln_b,
           w_out, ln1_g, ln1_b, w_gate, w_up, w_down, ln2_g, ln2_b, alpha):
    n_tok, d_model = x2d.shape
    d_sgu = sgu_ln_g.shape[0]
    d_conv = conv_ln_g.shape[0]
    n_heads, chunk, _ = w_s.shape
    conv_width = conv_w.shape[0]
    d_ff = w_gate.shape[1]
    tm = TM
    assert chunk == CHUNK and n_heads * HEAD_DIM == d_sgu and d_sgu % LANES == 0
    assert conv_width - 1 <= HALO and d_ff % FF_CHUNK == 0 and d_model % MXU_COLS == 0
    assert seq % tm == 0 and tm % CHUNK == 0 and tm % CONV_ROWS == 0
    n_tiles = n_tok // tm

    bf16 = jnp.bfloat16
    row = lambda p: p.reshape(1, -1)
    ws_pairs = w_s.reshape(n_heads // 2, 2, chunk, chunk).transpose(0, 2, 1, 3).reshape(
        n_heads // 2, chunk, 2 * chunk).astype(bf16)
    bias = jnp.repeat(b_s.T, HEAD_DIM, axis=1)
    params = pltpu.CompilerParams(dimension_semantics=("arbitrary",), vmem_limit_bytes=VMEM_LIMIT_BYTES)

    tile = lambda width: pl.BlockSpec((tm, width), lambda i: (i, 0))
    h, ya = pl.pallas_call(
        _front_kernel,
        grid=(n_tiles,),
        in_specs=[
            tile(d_model),
            _const_spec((d_model, 2 * d_sgu + 2 * d_conv)),
            _const_spec((1, d_sgu)), _const_spec((1, d_sgu)),
            _const_spec((n_heads // 2, chunk, 2 * chunk)),
            _const_spec((chunk, d_sgu)),
        ],
        out_specs=[tile(d_conv), tile(d_sgu)],
        out_shape=[jax.ShapeDtypeStruct((n_tok, d_conv), jnp.float32),
                   jax.ShapeDtypeStruct((n_tok, d_sgu), bf16)],
        compiler_params=params,
        name="front",
    )(x2d, w_in.astype(bf16), row(sgu_ln_g), row(sgu_ln_b), ws_pairs, bias)

    cur = lambda width: pl.BlockSpec((tm, width), lambda i: (jnp.minimum(i, n_tiles - 1), 0))
    prev = pl.BlockSpec((tm, d_model), lambda i: (jnp.maximum(i - 1, 0), 0))
    return pl.pallas_call(
        functools.partial(_back_kernel, tiles_per_seq=seq // tm, alpha=alpha),
        grid=(n_tiles + 1,),
        in_specs=[
            cur(d_conv), cur(d_sgu), cur(d_model),
            _const_spec((conv_width, d_conv)),
            _const_spec((1, d_conv)), _const_spec((1, d_conv)), _const_spec((1, d_conv)),
            _const_spec((d_sgu + d_conv, d_model)),
            _const_spec((1, d_model)), _const_spec((1, d_model)),
            _const_spec((d_model, d_ff)), _const_spec((d_model, d_ff)), _const_spec((d_ff, d_model)),
            _const_spec((1, d_model)), _const_spec((1, d_model)),
        ],
        out_specs=prev,
        out_shape=jax.ShapeDtypeStruct((n_tok, d_model), jnp.float32),
        scratch_shapes=[
            pltpu.VMEM((tm, d_model), jnp.float32),
            pltpu.VMEM((HALO + tm, d_conv), jnp.float32),
            pltpu.VMEM((tm, d_conv), jnp.float32),
            pltpu.VMEM((SUBLANES, CONV_ROWS + HALO, LANES), jnp.float32),
            pltpu.VMEM((tm, d_sgu + d_conv), bf16),
            pltpu.VMEM((tm, d_ff), bf16),
            pltpu.VMEM((tm, d_model), jnp.float32),
        ],
        compiler_params=params,
        name="back",
    )(h, ya, x2d, conv_w, row(conv_b), row(conv_ln_g), row(conv_ln_b), w_out.astype(bf16),
      row(ln1_g), row(ln1_b), w_gate.astype(bf16), w_up.astype(bf16), w_down.astype(bf16),
      row(ln2_g), row(ln2_b))


def kernel(x, w_in, sgu_ln_g, sgu_ln_b, w_s, b_s, conv_w, conv_b, conv_ln_g, conv_ln_b, w_out, ln1_g, ln1_b, w_gate, w_up, w_down, ln2_g, ln2_b):
    batch, seq, d_model = x.shape
    depth = w_in.shape[0]
    alpha = (2.0 * depth) ** 0.25
    x2d = x.reshape(batch * seq, d_model)
    for l in range(depth):
        x2d = _layer(x2d, seq, w_in[l], sgu_ln_g[l], sgu_ln_b[l], w_s[l], b_s[l], conv_w[l], conv_b[l],
                     conv_ln_g[l], conv_ln_b[l], w_out[l], ln1_g[l], ln1_b[l],
                     w_gate[l], w_up[l], w_down[l], ln2_g[l], ln2_b[l], alpha)
    return x2d.reshape(batch, seq, d_model)
```

```python
import functools

import jax
import jax.numpy as jnp
from jax import lax
from jax.experimental import pallas as pl
from jax.experimental.pallas import tpu as pltpu

HEAD_DIM = 64
CHUNK = 128
LN_EPS = 1e-5
LANES = 128
SUBLANES = 8
MXU_COLS = 256
HALO = 32
CONV_ROWS = 64
ACC_ROWS = 16
FF_CHUNK = MXU_COLS
UNITS_PER_CHUNK = 3
GATE_ROWS = 16
LAG = 2
TM = 512
TM_FRONT = 1024
VMEM_LIMIT_BYTES = 56 * 1024 * 1024


def _layer_norm(x, g, b):
    mu = jnp.mean(x, axis=-1, keepdims=True)
    xc = x - mu
    var = jnp.mean(xc * xc, axis=-1, keepdims=True)
    return xc * lax.rsqrt(var + LN_EPS) * g + b


def _gelu(x):
    return 0.5 * x * (1.0 + lax.erf(x * (0.5 ** 0.5)))


def _dot(a, b):
    return jnp.dot(a, b, preferred_element_type=jnp.float32)


def _zero_bits(dep):
    bits = pltpu.bitcast(dep[0:SUBLANES, 0:LANES], jnp.uint32)
    return lax.shift_right_logical(lax.shift_right_logical(bits, jnp.uint32(16)), jnp.uint32(16))


def _after(x, dep):
    zero = jnp.tile(_zero_bits(dep), (x.shape[0] // SUBLANES, x.shape[1] // LANES))
    return jnp.where(zero == 0, x, 0.0)


def _front_kernel(x_ref, win_ref, sg_ref, sb_ref, ws_ref, bs_ref, h_ref, ya_ref):
    tm = x_ref.shape[0]
    d_sgu = sg_ref.shape[1]
    d_conv = h_ref.shape[1]
    n_chunks = tm // CHUNK
    xb = x_ref[...].astype(jnp.bfloat16)

    a = _dot(xb, win_ref[:, 2 * d_sgu:2 * d_sgu + d_conv])
    g = _dot(xb, win_ref[:, 2 * d_sgu + d_conv:2 * d_sgu + 2 * d_conv])
    h_ref[...] = a * jax.nn.sigmoid(g)

    u = _gelu(_dot(xb, win_ref[:, 0:d_sgu]))
    v = _gelu(_dot(xb, win_ref[:, d_sgu:2 * d_sgu]))
    v = _layer_norm(v, sg_ref[...], sb_ref[...]).astype(jnp.bfloat16)

    row = lax.broadcasted_iota(jnp.int32, (CHUNK, 2 * CHUNK), 0)
    col = lax.broadcasted_iota(jnp.int32, (CHUNK, 2 * CHUNK), 1)
    causal = (col % CHUNK) <= row
    lane = lax.broadcasted_iota(jnp.int32, (CHUNK, LANES), 1)
    low_head = lane < HEAD_DIM
    zero = jnp.zeros((CHUNK, LANES), jnp.bfloat16)
    for j in range(d_sgu // LANES):
        w_pair = jnp.where(causal, ws_ref[j], jnp.zeros_like(ws_ref[j]))
        bias = bs_ref[:, j * LANES:(j + 1) * LANES]
        for c0 in range(0, n_chunks, 2):
            cs = list(range(c0, min(c0 + 2, n_chunks)))
            rhs = []
            for c in cs:
                vj = v[c * CHUNK:(c + 1) * CHUNK, j * LANES:(j + 1) * LANES]
                rhs.append(jnp.concatenate(
                    [jnp.where(low_head, vj, zero), jnp.where(low_head, zero, vj)], axis=0))
            mixed = _dot(w_pair, jnp.concatenate(rhs, axis=1) if len(rhs) > 1 else rhs[0])
            for i, c in enumerate(cs):
                uc = u[c * CHUNK:(c + 1) * CHUNK, j * LANES:(j + 1) * LANES]
                out_a = uc * (mixed[:, i * LANES:(i + 1) * LANES] + bias)
                ya_ref[c * CHUNK:(c + 1) * CHUNK, j * LANES:(j + 1) * LANES] = out_a.astype(jnp.bfloat16)


def _conv_unit(unit, hext_ref, cw_ref, yconv_ref, rot_ref, after):
    conv_width, d_conv = cw_ref.shape
    n_lane = d_conv // LANES
    t0 = (unit // n_lane) * CONV_ROWS
    lanes = slice((unit % n_lane) * LANES, (unit % n_lane + 1) * LANES)
    base = HALO - (conv_width - 1)
    span = CONV_ROWS + HALO
    src = _after(hext_ref[t0:t0 + span, lanes], after)
    rot_ref[0] = src
    for r in range(1, SUBLANES):
        rot_ref[r] = pltpu.roll(src, span - r, 0)
    acc = None
    for i0 in range(0, CONV_ROWS, ACC_ROWS):
        prev, acc = acc, None
        for o in range(base, base + conv_width):
            r = o % SUBLANES
            term = rot_ref[r, o - r + i0:o - r + i0 + ACC_ROWS, :] * cw_ref[o - base:o - base + 1, lanes]
            if acc is None:
                acc = term if prev is None else _after(term, prev)
            else:
                acc = acc + term
        yconv_ref[t0 + i0:t0 + i0 + ACC_ROWS, lanes] = acc
    return acc


def _gated_lhs(lhs, head_f32, token):
    head = _after(head_f32, token).astype(jnp.bfloat16)
    rows = head.shape[0]
    return jnp.concatenate(
        [jnp.concatenate([head, lhs[rows:, 0:MXU_COLS]], axis=0), lhs[:, MXU_COLS:]], axis=1)


def _back_kernel(h_ref, ya_ref, x_ref, cw_ref, cb_ref, cg_ref, cbb_ref, wout_ref, g1_ref, b1_ref,
                 wg_ref, wu_ref, wd_ref, g2_ref, b2_ref, o_ref,
                 x1_ref, hext_ref, yconv_ref, rot_ref, ycat_ref, hid_ref, res_ref, *, tiles_per_seq, alpha):
    tm, d_model = x_ref.shape
    d_conv = h_ref.shape[1]
    d_sgu = ya_ref.shape[1]
    d_ff = wg_ref.shape[1]
    n_chunks = d_ff // FF_CHUNK
    n_units = (tm // CONV_ROWS) * (d_conv // LANES)
    step = pl.program_id(0)

    @pl.when(step == 0)
    def _():
        x1_ref[...] = jnp.zeros_like(x1_ref)
        hext_ref[tm:tm + HALO, :] = jnp.zeros((HALO, d_conv), jnp.float32)

    seq_start = (step % tiles_per_seq) == 0
    hext_ref[0:HALO, :] = jnp.where(seq_start, 0.0, hext_ref[tm:tm + HALO, :])
    hext_ref[HALO:HALO + tm, :] = h_ref[...]

    xfb = x1_ref[...].astype(jnp.bfloat16)
    res_ref[...] = alpha * x1_ref[...]
    unit = 0
    done = []
    for ci in range(n_chunks):
        cols = slice(ci * FF_CHUNK, (ci + 1) * FF_CHUNK)
        lhs = xfb if ci < LAG else _gated_lhs(xfb, x1_ref[0:GATE_ROWS, 0:MXU_COLS], done[ci - LAG])
        gate = _dot(lhs, wg_ref[:, cols])
        up = _dot(lhs, wu_ref[:, cols])
        hid_ref[:, cols] = (gate * jax.nn.sigmoid(gate) * up).astype(jnp.bfloat16)
        token = gate
        for _ in range(UNITS_PER_CHUNK if ci < n_chunks - 1 else n_units - unit):
            token = _conv_unit(unit, hext_ref, cw_ref, yconv_ref, rot_ref, token)
            unit += 1
        done.append(token)

    def down_half(half, token, r0=0, nrows=tm):
        rows = slice(r0, r0 + nrows)
        lhs = _gated_lhs(hid_ref[rows, :],
                         hid_ref[r0:r0 + GATE_ROWS, 0:MXU_COLS].astype(jnp.float32), token)
        for n in range(2 * half, 2 * half + 2):
            cols = slice(n * (d_model // 4), (n + 1) * (d_model // 4))
            res_ref[rows, cols] += _dot(lhs, wd_ref[:, cols])

    down_half(0, done[n_chunks - LAG])

    y = yconv_ref[...] + cb_ref[...]
    y = _layer_norm(y, cg_ref[...], cbb_ref[...])
    y = y * jax.nn.sigmoid(y)
    ycat_ref[:, 0:d_sgu] = ya_ref[...]
    ycat_ref[:, d_sgu:d_sgu + d_conv] = y.astype(jnp.bfloat16)
    r1 = alpha * x_ref[...] + _dot(ycat_ref[...], wout_ref[...])
    x1_ref[...] = _layer_norm(r1, g1_ref[...], b1_ref[...])

    for r0 in range(0, tm, tm // 2):
        down_half(1, r1, r0, tm // 2)
        rows = slice(r0, r0 + tm // 2)
        o_ref[rows, :] = _layer_norm(res_ref[rows, :], g2_ref[...], b2_ref[...])


def _const_spec(shape):
    return pl.BlockSpec(shape, lambda i: (0,) * len(shape), pipeline_mode=pl.Buffered(1))


def _layer(x2d, seq, w_in, sgu_ln_g, sgu_ln_b, w_s, b_s, conv_w, conv_b, conv_ln_g, conv_ln_b,
           w_out, ln1_g, ln1_b, w_gate, w_up, w_down, ln2_g, ln2_b, alpha):
    n_tok, d_model = x2d.shape
    d_sgu = sgu_ln_g.shape[0]
    d_conv = conv_ln_g.shape[0]
    n_heads, chunk, _ = w_s.shape
    conv_width = conv_w.shape[0]
    d_ff = w_gate.shape[1]
    tm = TM
    assert chunk == CHUNK and n_heads * HEAD_DIM == d_sgu and d_sgu % LANES == 0
    assert conv_width - 1 <= HALO and d_ff % FF_CHUNK == 0 and d_model % MXU_COLS == 0
    assert seq % tm == 0 and tm % CHUNK == 0 and tm % CONV_ROWS == 0
    assert seq % CHUNK == 0 and TM_FRONT % CHUNK == 0 and n_tok % TM_FRONT == 0
    n_tiles = n_tok // tm

    bf16 = jnp.bfloat16
    row = lambda p: p.reshape(1, -1)
    ws_pairs = w_s.reshape(n_heads // 2, 2, chunk, chunk).transpose(0, 2, 1, 3).reshape(
        n_heads // 2, chunk, 2 * chunk).astype(bf16)
    bias = jnp.repeat(b_s.T, HEAD_DIM, axis=1)
    params = pltpu.CompilerParams(dimension_semantics=("arbitrary",), vmem_limit_bytes=VMEM_LIMIT_BYTES)

    tile = lambda width: pl.BlockSpec((TM_FRONT, width), lambda i: (i, 0))
    h, ya = pl.pallas_call(
        _front_kernel,
        grid=(n_tok // TM_FRONT,),
        in_specs=[
            tile(d_model),
            _const_spec((d_model, 2 * d_sgu + 2 * d_conv)),
            _const_spec((1, d_sgu)), _const_spec((1, d_sgu)),
            _const_spec((n_heads // 2, chunk, 2 * chunk)),
            _const_spec((chunk, d_sgu)),
        ],
        out_specs=[tile(d_conv), tile(d_sgu)],
        out_shape=[jax.ShapeDtypeStruct((n_tok, d_conv), jnp.float32),
                   jax.ShapeDtypeStruct((n_tok, d_sgu), bf16)],
        compiler_params=params,
        name="front",
    )(x2d, w_in.astype(bf16), row(sgu_ln_g), row(sgu_ln_b), ws_pairs, bias)

    cur = lambda width: pl.BlockSpec((tm, width), lambda i: (jnp.minimum(i, n_tiles - 1), 0))
    prev = pl.BlockSpec((tm, d_model), lambda i: (jnp.maximum(i - 1, 0), 0))
    return pl.pallas_call(
        functools.partial(_back_kernel, tiles_per_seq=seq // tm, alpha=alpha),
        grid=(n_tiles + 1,),
        in_specs=[
            cur(d_conv), cur(d_sgu), cur(d_model),
            _const_spec((conv_width, d_conv)),
            _const_spec((1, d_conv)), _const_spec((1, d_conv)), _const_spec((1, d_conv)),
            _const_spec((d_sgu + d_conv, d_model)),
            _const_spec((1, d_model)), _const_spec((1, d_model)),
            _const_spec((d_model, d_ff)), _const_spec((d_model, d_ff)), _const_spec((d_ff, d_model)),
            _const_spec((1, d_model)), _const_spec((1, d_model)),
        ],
        out_specs=prev,
        out_shape=jax.ShapeDtypeStruct((n_tok, d_model), jnp.float32),
        scratch_shapes=[
            pltpu.VMEM((tm, d_model), jnp.float32),
            pltpu.VMEM((HALO + tm, d_conv), jnp.float32),
            pltpu.VMEM((tm, d_conv), jnp.float32),
            pltpu.VMEM((SUBLANES, CONV_ROWS + HALO, LANES), jnp.float32),
            pltpu.VMEM((tm, d_sgu + d_conv), bf16),
            pltpu.VMEM((tm, d_ff), bf16),
            pltpu.VMEM((tm, d_model), jnp.float32),
        ],
        compiler_params=params,
        name="back",
    )(h, ya, x2d, conv_w, row(conv_b), row(conv_ln_g), row(conv_ln_b), w_out.astype(bf16),
      row(ln1_g), row(ln1_b), w_gate.astype(bf16), w_up.astype(bf16), w_down.astype(bf16),
      row(ln2_g), row(ln2_b))


def kernel(x, w_in, sgu_ln_g, sgu_ln_b, w_s, b_s, conv_w, conv_b, conv_ln_g, conv_ln_b, w_out, ln1_g, ln1_b, w_gate, w_up, w_down, ln2_g, ln2_b):
    batch, seq, d_model = x.shape
    depth = w_in.shape[0]
    alpha = (2.0 * depth) ** 0.25
    x2d = x.reshape(batch * seq, d_model)
    for l in range(depth):
        x2d = _layer(x2d, seq, w_in[l], sgu_ln_g[l], sgu_ln_b[l], w_s[l], b_s[l], conv_w[l], conv_b[l],
                     conv_ln_g[l], conv_ln_b[l], w_out[l], ln1_g[l], ln1_b[l],
                     w_gate[l], w_up[l], w_down[l], ln2_g[l], ln2_b[l], alpha)
    return x2d.reshape(batch, seq, d_model)
```

```python
import functools

import jax
import jax.numpy as jnp
from jax import lax
from jax.experimental import pallas as pl
from jax.experimental.pallas import tpu as pltpu

HEAD_DIM = 64
CHUNK = 128
LN_EPS = 1e-5
LANES = 128
SUBLANES = 8
MXU_COLS = 256
HALO = 32
CONV_ROWS = 32
ACC_ROWS = 16
FF_CHUNK = MXU_COLS
GATE_ROWS = 16
LAG = 2
TM = 512
TM_FRONT = 1024
VMEM_LIMIT_BYTES = 56 * 1024 * 1024


def _layer_norm(x, g, b):
    mu = jnp.mean(x, axis=-1, keepdims=True)
    xc = x - mu
    var = jnp.mean(xc * xc, axis=-1, keepdims=True)
    return xc * lax.rsqrt(var + LN_EPS) * g + b


def _gelu(x):
    return 0.5 * x * (1.0 + lax.erf(x * (0.5 ** 0.5)))


def _silu(x):
    hx = 0.5 * x
    return hx + hx * jnp.tanh(hx)


def _dot(a, b):
    return jnp.dot(a, b, preferred_element_type=jnp.float32)


def _zero_bits(dep):
    bits = pltpu.bitcast(dep[0:SUBLANES, 0:LANES], jnp.uint32)
    return lax.shift_right_logical(lax.shift_right_logical(bits, jnp.uint32(16)), jnp.uint32(16))


def _after(x, dep):
    zero = jnp.tile(_zero_bits(dep), (x.shape[0] // SUBLANES, x.shape[1] // LANES))
    return jnp.where(zero == 0, x, 0.0)


def _front_kernel(x_ref, win_ref, sg_ref, sb_ref, ws_ref, bs_ref, h_ref, ya_ref):
    tm = x_ref.shape[0]
    d_sgu = sg_ref.shape[1]
    d_conv = h_ref.shape[1]
    n_chunks = tm // CHUNK
    xb = x_ref[...].astype(jnp.bfloat16)

    a = _dot(xb, win_ref[:, 2 * d_sgu:2 * d_sgu + d_conv])
    g = _dot(xb, win_ref[:, 2 * d_sgu + d_conv:2 * d_sgu + 2 * d_conv])
    h_ref[...] = a * jax.nn.sigmoid(g)

    u = _gelu(_dot(xb, win_ref[:, 0:d_sgu]))
    v = _gelu(_dot(xb, win_ref[:, d_sgu:2 * d_sgu]))
    v = _layer_norm(v, sg_ref[...], sb_ref[...]).astype(jnp.bfloat16)

    row = lax.broadcasted_iota(jnp.int32, (CHUNK, 2 * CHUNK), 0)
    col = lax.broadcasted_iota(jnp.int32, (CHUNK, 2 * CHUNK), 1)
    causal = (col % CHUNK) <= row
    lane = lax.broadcasted_iota(jnp.int32, (CHUNK, LANES), 1)
    low_head = lane < HEAD_DIM
    zero = jnp.zeros((CHUNK, LANES), jnp.bfloat16)
    for j in range(d_sgu // LANES):
        w_pair = jnp.where(causal, ws_ref[j], jnp.zeros_like(ws_ref[j]))
        bias = bs_ref[:, j * LANES:(j + 1) * LANES]
        for c0 in range(0, n_chunks, 2):
            cs = list(range(c0, min(c0 + 2, n_chunks)))
            rhs = []
            for c in cs:
                vj = v[c * CHUNK:(c + 1) * CHUNK, j * LANES:(j + 1) * LANES]
                rhs.append(jnp.concatenate(
                    [jnp.where(low_head, vj, zero), jnp.where(low_head, zero, vj)], axis=0))
            mixed = _dot(w_pair, jnp.concatenate(rhs, axis=1) if len(rhs) > 1 else rhs[0])
            for i, c in enumerate(cs):
                uc = u[c * CHUNK:(c + 1) * CHUNK, j * LANES:(j + 1) * LANES]
                out_a = uc * (mixed[:, i * LANES:(i + 1) * LANES] + bias)
                ya_ref[c * CHUNK:(c + 1) * CHUNK, j * LANES:(j + 1) * LANES] = out_a.astype(jnp.bfloat16)


def _conv_unit(unit, hext_ref, cw_ref, yconv_ref, rot_ref, after):
    conv_width = cw_ref.shape[0]
    cb, rb = divmod(unit, yconv_ref.shape[0] // CONV_ROWS)
    t0 = rb * CONV_ROWS
    lanes = slice(cb * LANES, (cb + 1) * LANES)
    base = HALO - (conv_width - 1)
    reach = max(o - o % SUBLANES for o in range(base, base + conv_width) if o % SUBLANES)
    lo, hi = (0 if rb == 0 else t0 + reach), t0 + CONV_ROWS + reach
    src = _after(hext_ref[lo:hi + SUBLANES, lanes], after)
    for r in range(1, SUBLANES):
        rot_ref[r - 1, lo:hi, :] = pltpu.roll(src, hi - lo + SUBLANES - r, 0)[0:hi - lo, :]
    acc = None
    for i0 in range(0, CONV_ROWS, ACC_ROWS):
        prev, acc = acc, None
        for o in range(base, base + conv_width):
            r = o % SUBLANES
            start = t0 + i0 + o - r
            if r == 0:
                win = hext_ref[start:start + ACC_ROWS, lanes]
            else:
                win = rot_ref[r - 1, start:start + ACC_ROWS, :]
            term = win * cw_ref[o - base:o - base + 1, lanes]
            if acc is None:
                acc = term if prev is None else _after(term, prev)
            else:
                acc = acc + term
        yconv_ref[t0 + i0:t0 + i0 + ACC_ROWS, lanes] = acc
    return acc


def _gated_lhs(lhs, head_f32, token):
    head = _after(head_f32, token).astype(jnp.bfloat16)
    rows = head.shape[0]
    return jnp.concatenate(
        [jnp.concatenate([head, lhs[rows:, 0:MXU_COLS]], axis=0), lhs[:, MXU_COLS:]], axis=1)


def _back_kernel(h_ref, ya_ref, x_ref, cw_ref, cb_ref, cg_ref, cbb_ref, wout_ref, g1_ref, b1_ref,
                 wg_ref, wu_ref, wd_ref, g2_ref, b2_ref, o_ref,
                 x1_ref, hext_ref, yconv_ref, rot_ref, ycat_ref, hid_ref, res_ref, *, tiles_per_seq, alpha):
    tm, d_model = x_ref.shape
    d_conv = h_ref.shape[1]
    d_sgu = ya_ref.shape[1]
    d_ff = wg_ref.shape[1]
    n_chunks = d_ff // FF_CHUNK
    n_units = (tm // CONV_ROWS) * (d_conv // LANES)
    step = pl.program_id(0)

    @pl.when(step == 0)
    def _():
        x1_ref[...] = jnp.zeros_like(x1_ref)
        hext_ref[tm:tm + HALO, :] = jnp.zeros((HALO, d_conv), jnp.float32)

    seq_start = (step % tiles_per_seq) == 0
    hext_ref[0:HALO, :] = jnp.where(seq_start, 0.0, hext_ref[tm:tm + HALO, :])
    hext_ref[HALO:HALO + tm, :] = h_ref[...]

    xfb = x1_ref[...].astype(jnp.bfloat16)
    res_ref[...] = alpha * x1_ref[...]
    unit = 0
    done = []
    for ci in range(n_chunks):
        cols = slice(ci * FF_CHUNK, (ci + 1) * FF_CHUNK)
        lhs = xfb if ci < LAG else _gated_lhs(xfb, x1_ref[0:GATE_ROWS, 0:MXU_COLS], done[ci - LAG])
        gate = _dot(lhs, wg_ref[:, cols])
        up = _dot(lhs, wu_ref[:, cols])
        hid_ref[:, cols] = (_silu(gate) * up).astype(jnp.bfloat16)
        token = gate
        for _ in range(-(-(ci + 1) * n_units // n_chunks) - unit):
            token = _conv_unit(unit, hext_ref, cw_ref, yconv_ref, rot_ref, token)
            unit += 1
        done.append(token)

    def down_half(half, token, r0=0, nrows=tm):
        rows = slice(r0, r0 + nrows)
        lhs = _gated_lhs(hid_ref[rows, :],
                         hid_ref[r0:r0 + GATE_ROWS, 0:MXU_COLS].astype(jnp.float32), token)
        for n in range(2 * half, 2 * half + 2):
            cols = slice(n * (d_model // 4), (n + 1) * (d_model // 4))
            res_ref[rows, cols] += _dot(lhs, wd_ref[:, cols])

    down_half(0, done[n_chunks - LAG])

    y = yconv_ref[...] + cb_ref[...]
    y = _layer_norm(y, cg_ref[...], cbb_ref[...])
    y = _silu(y)
    ycat_ref[:, 0:d_sgu] = ya_ref[...]
    ycat_ref[:, d_sgu:d_sgu + d_conv] = y.astype(jnp.bfloat16)
    r1 = alpha * x_ref[...] + _dot(ycat_ref[...], wout_ref[...])
    x1_ref[...] = _layer_norm(r1, g1_ref[...], b1_ref[...])

    for r0 in range(0, tm, tm // 2):
        down_half(1, r1, r0, tm // 2)
        rows = slice(r0, r0 + tm // 2)
        o_ref[rows, :] = _layer_norm(res_ref[rows, :], g2_ref[...], b2_ref[...])


def _const_spec(shape):
    return pl.BlockSpec(shape, lambda i: (0,) * len(shape), pipeline_mode=pl.Buffered(1))


def _layer(x2d, seq, w_in, sgu_ln_g, sgu_ln_b, w_s, b_s, conv_w, conv_b, conv_ln_g, conv_ln_b,
           w_out, ln1_g, ln1_b, w_gate, w_up, w_down, ln2_g, ln2_b, alpha):
    n_tok, d_model = x2d.shape
    d_sgu = sgu_ln_g.shape[0]
    d_conv = conv_ln_g.shape[0]
    n_heads, chunk, _ = w_s.shape
    conv_width = conv_w.shape[0]
    d_ff = w_gate.shape[1]
    tm = TM
    assert chunk == CHUNK and n_heads * HEAD_DIM == d_sgu and d_sgu % LANES == 0
    assert conv_width - 1 <= HALO and d_ff % FF_CHUNK == 0 and d_model % MXU_COLS == 0
    assert seq % tm == 0 and tm % CHUNK == 0 and tm % CONV_ROWS == 0
    assert seq % CHUNK == 0 and TM_FRONT % CHUNK == 0 and n_tok % TM_FRONT == 0
    n_tiles = n_tok // tm

    bf16 = jnp.bfloat16
    row = lambda p: p.reshape(1, -1)
    ws_pairs = w_s.reshape(n_heads // 2, 2, chunk, chunk).transpose(0, 2, 1, 3).reshape(
        n_heads // 2, chunk, 2 * chunk).astype(bf16)
    bias = jnp.repeat(b_s.T, HEAD_DIM, axis=1)
    params = pltpu.CompilerParams(dimension_semantics=("arbitrary",), vmem_limit_bytes=VMEM_LIMIT_BYTES)

    tile = lambda width: pl.BlockSpec((TM_FRONT, width), lambda i: (i, 0))
    h, ya = pl.pallas_call(
        _front_kernel,
        grid=(n_tok // TM_FRONT,),
        in_specs=[
            tile(d_model),
            _const_spec((d_model, 2 * d_sgu + 2 * d_conv)),
            _const_spec((1, d_sgu)), _const_spec((1, d_sgu)),
            _const_spec((n_heads // 2, chunk, 2 * chunk)),
            _const_spec((chunk, d_sgu)),
        ],
        out_specs=[tile(d_conv), tile(d_sgu)],
        out_shape=[jax.ShapeDtypeStruct((n_tok, d_conv), jnp.float32),
                   jax.ShapeDtypeStruct((n_tok, d_sgu), bf16)],
        compiler_params=params,
        name="front",
    )(x2d, w_in.astype(bf16), row(sgu_ln_g), row(sgu_ln_b), ws_pairs, bias)

    cur = lambda width: pl.BlockSpec((tm, width), lambda i: (jnp.minimum(i, n_tiles - 1), 0))
    prev = pl.BlockSpec((tm, d_model), lambda i: (jnp.maximum(i - 1, 0), 0))
    return pl.pallas_call(
        functools.partial(_back_kernel, tiles_per_seq=seq // tm, alpha=alpha),
        grid=(n_tiles + 1,),
        in_specs=[
            cur(d_conv), cur(d_sgu), cur(d_model),
            _const_spec((conv_width, d_conv)),
            _const_spec((1, d_conv)), _const_spec((1, d_conv)), _const_spec((1, d_conv)),
            _const_spec((d_sgu + d_conv, d_model)),
            _const_spec((1, d_model)), _const_spec((1, d_model)),
            _const_spec((d_model, d_ff)), _const_spec((d_model, d_ff)), _const_spec((d_ff, d_model)),
            _const_spec((1, d_model)), _const_spec((1, d_model)),
        ],
        out_specs=prev,
        out_shape=jax.ShapeDtypeStruct((n_tok, d_model), jnp.float32),
        scratch_shapes=[
            pltpu.VMEM((tm, d_model), jnp.float32),
            pltpu.VMEM((HALO + tm, d_conv), jnp.float32),
            pltpu.VMEM((tm, d_conv), jnp.float32),
            pltpu.VMEM((SUBLANES - 1, HALO + tm, LANES), jnp.float32),
            pltpu.VMEM((tm, d_sgu + d_conv), bf16),
            pltpu.VMEM((tm, d_ff), bf16),
            pltpu.VMEM((tm, d_model), jnp.float32),
        ],
        compiler_params=params,
        name="back",
    )(h, ya, x2d, conv_w, row(conv_b), row(conv_ln_g), row(conv_ln_b), w_out.astype(bf16),
      row(ln1_g), row(ln1_b), w_gate.astype(bf16), w_up.astype(bf16), w_down.astype(bf16),
      row(ln2_g), row(ln2_b))


def kernel(x, w_in, sgu_ln_g, sgu_ln_b, w_s, b_s, conv_w, conv_b, conv_ln_g, conv_ln_b, w_out, ln1_g, ln1_b, w_gate, w_up, w_down, ln2_g, ln2_b):
    batch, seq, d_model = x.shape
    depth = w_in.shape[0]
    alpha = (2.0 * depth) ** 0.25
    x2d = x.reshape(batch * seq, d_model)
    for l in range(depth):
        x2d = _layer(x2d, seq, w_in[l], sgu_ln_g[l], sgu_ln_b[l], w_s[l], b_s[l], conv_w[l], conv_b[l],
                     conv_ln_g[l], conv_ln_b[l], w_out[l], ln1_g[l], ln1_b[l],
                     w_gate[l], w_up[l], w_down[l], ln2_g[l], ln2_b[l], alpha)
    return x2d.reshape(batch, seq, d_model)
```

```python
import functools

import jax
import jax.numpy as jnp
from jax import lax
from jax.experimental import pallas as pl
from jax.experimental.pallas import tpu as pltpu

HEAD_DIM = 64
CHUNK = 128
LN_EPS = 1e-5
LANES = 128
SUBLANES = 8
MXU_COLS = 256
HALO = 32
CONV_ROWS = 32
ACC_ROWS = 16
FF_CHUNK = MXU_COLS
GATE_ROWS = 16
LN_ROWS = 8
LN_DEPTH = 8
LAG = 3
TM = 512
TM_FRONT = 1024
VMEM_LIMIT_BYTES = 56 * 1024 * 1024


def _layer_norm(x, g, b):
    mu = jnp.mean(x, axis=-1, keepdims=True)
    xc = x - mu
    var = jnp.mean(xc * xc, axis=-1, keepdims=True)
    return xc * lax.rsqrt(var + LN_EPS) * g + b


def _gelu(x):
    return 0.5 * x * (1.0 + lax.erf(x * (0.5 ** 0.5)))


def _silu(x):
    hx = 0.5 * x
    return hx + hx * jnp.tanh(hx)


def _dot(a, b):
    return jnp.dot(a, b, preferred_element_type=jnp.float32)


def _zero_bits(dep):
    bits = pltpu.bitcast(dep[0:SUBLANES, 0:LANES], jnp.uint32)
    return lax.shift_right_logical(lax.shift_right_logical(bits, jnp.uint32(16)), jnp.uint32(16))


def _after(x, dep):
    zero = jnp.tile(_zero_bits(dep), (x.shape[0] // SUBLANES, x.shape[1] // LANES))
    return jnp.where(zero == 0, x, 0.0)


def _front_kernel(x_ref, win_ref, sg_ref, sb_ref, ws_ref, bs_ref, h_ref, ya_ref):
    tm = x_ref.shape[0]
    d_sgu = sg_ref.shape[1]
    d_conv = h_ref.shape[1]
    n_chunks = tm // CHUNK
    xb = x_ref[...].astype(jnp.bfloat16)

    a = _dot(xb, win_ref[:, 2 * d_sgu:2 * d_sgu + d_conv])
    g = _dot(xb, win_ref[:, 2 * d_sgu + d_conv:2 * d_sgu + 2 * d_conv])
    h_ref[...] = a * jax.nn.sigmoid(g)

    u = _gelu(_dot(xb, win_ref[:, 0:d_sgu]))
    v = _gelu(_dot(xb, win_ref[:, d_sgu:2 * d_sgu]))
    v = _layer_norm(v, sg_ref[...], sb_ref[...]).astype(jnp.bfloat16)

    row = lax.broadcasted_iota(jnp.int32, (CHUNK, 2 * CHUNK), 0)
    col = lax.broadcasted_iota(jnp.int32, (CHUNK, 2 * CHUNK), 1)
    causal = (col % CHUNK) <= row
    lane = lax.broadcasted_iota(jnp.int32, (CHUNK, LANES), 1)
    low_head = lane < HEAD_DIM
    zero = jnp.zeros((CHUNK, LANES), jnp.bfloat16)
    for j in range(d_sgu // LANES):
        w_pair = jnp.where(causal, ws_ref[j], jnp.zeros_like(ws_ref[j]))
        bias = bs_ref[:, j * LANES:(j + 1) * LANES]
        for c0 in range(0, n_chunks, 2):
            cs = list(range(c0, min(c0 + 2, n_chunks)))
            rhs = []
            for c in cs:
                vj = v[c * CHUNK:(c + 1) * CHUNK, j * LANES:(j + 1) * LANES]
                rhs.append(jnp.concatenate(
                    [jnp.where(low_head, vj, zero), jnp.where(low_head, zero, vj)], axis=0))
            mixed = _dot(w_pair, jnp.concatenate(rhs, axis=1) if len(rhs) > 1 else rhs[0])
            for i, c in enumerate(cs):
                uc = u[c * CHUNK:(c + 1) * CHUNK, j * LANES:(j + 1) * LANES]
                out_a = uc * (mixed[:, i * LANES:(i + 1) * LANES] + bias)
                ya_ref[c * CHUNK:(c + 1) * CHUNK, j * LANES:(j + 1) * LANES] = out_a.astype(jnp.bfloat16)


def _conv_unit(unit, hext_ref, cw_ref, yconv_ref, rot_ref, after):
    conv_width = cw_ref.shape[0]
    cb, rb = divmod(unit, yconv_ref.shape[0] // CONV_ROWS)
    t0 = rb * CONV_ROWS
    lanes = slice(cb * LANES, (cb + 1) * LANES)
    base = HALO - (conv_width - 1)
    reach = max(o - o % SUBLANES for o in range(base, base + conv_width) if o % SUBLANES)
    lo, hi = (0 if rb == 0 else t0 + reach), t0 + CONV_ROWS + reach
    src = _after(hext_ref[lo:hi + SUBLANES, lanes], after)
    for r in range(1, SUBLANES):
        rot_ref[r - 1, lo:hi, :] = pltpu.roll(src, hi - lo + SUBLANES - r, 0)[0:hi - lo, :]
    acc = None
    for i0 in range(0, CONV_ROWS, ACC_ROWS):
        prev, acc = acc, None
        for o in range(base, base + conv_width):
            r = o % SUBLANES
            start = t0 + i0 + o - r
            if r == 0:
                win = hext_ref[start:start + ACC_ROWS, lanes]
            else:
                win = rot_ref[r - 1, start:start + ACC_ROWS, :]
            term = win * cw_ref[o - base:o - base + 1, lanes]
            if acc is None:
                acc = term if prev is None else _after(term, prev)
            else:
                acc = acc + term
        yconv_ref[t0 + i0:t0 + i0 + ACC_ROWS, lanes] = acc
    return acc


def _gated_lhs(lhs, head_f32, token):
    head = _after(head_f32, token).astype(jnp.bfloat16)
    rows = head.shape[0]
    return jnp.concatenate(
        [jnp.concatenate([head, lhs[rows:, 0:MXU_COLS]], axis=0), lhs[:, MXU_COLS:]], axis=1)


def _back_kernel(h_ref, ya_ref, x_ref, cw_ref, cb_ref, cg_ref, cbb_ref, wout_ref, g1_ref, b1_ref,
                 wg_ref, wu_ref, wd_ref, g2_ref, b2_ref, o_ref,
                 x1_ref, hext_ref, yconv_ref, rot_ref, ycat_ref, hid_ref, res_ref, *, tiles_per_seq, alpha):
    tm, d_model = x_ref.shape
    d_conv = h_ref.shape[1]
    d_sgu = ya_ref.shape[1]
    d_ff = wg_ref.shape[1]
    n_chunks = d_ff // FF_CHUNK
    n_units = (tm // CONV_ROWS) * (d_conv // LANES)
    step = pl.program_id(0)

    @pl.when(step == 0)
    def _():
        x1_ref[...] = jnp.zeros_like(x1_ref)
        hext_ref[tm:tm + HALO, :] = jnp.zeros((HALO, d_conv), jnp.float32)

    seq_start = (step % tiles_per_seq) == 0
    hext_ref[0:HALO, :] = jnp.where(seq_start, 0.0, hext_ref[tm:tm + HALO, :])
    hext_ref[HALO:HALO + tm, :] = h_ref[...]

    xfb = x1_ref[...].astype(jnp.bfloat16)
    res_ref[...] = alpha * x1_ref[...]
    unit = 0
    done = []
    for ci in range(n_chunks):
        cols = slice(ci * FF_CHUNK, (ci + 1) * FF_CHUNK)
        lhs = xfb if ci < LAG else _gated_lhs(xfb, x1_ref[0:GATE_ROWS, 0:MXU_COLS], done[ci - LAG])
        gate = _dot(lhs, wg_ref[:, cols])
        up = _dot(lhs, wu_ref[:, cols])
        hid_ref[:, cols] = (_silu(gate) * up).astype(jnp.bfloat16)
        token = gate
        for _ in range(-(-(ci + 1) * n_units // n_chunks) - unit):
            token = _conv_unit(unit, hext_ref, cw_ref, yconv_ref, rot_ref, token)
            unit += 1
        done.append(token)

    def down_half(half, token, r0=0, nrows=tm):
        rows = slice(r0, r0 + nrows)
        lhs = _gated_lhs(hid_ref[rows, :],
                         hid_ref[r0:r0 + GATE_ROWS, 0:MXU_COLS].astype(jnp.float32), token)
        for n in range(2 * half, 2 * half + 2):
            cols = slice(n * (d_model // 4), (n + 1) * (d_model // 4))
            res_ref[rows, cols] += _dot(lhs, wd_ref[:, cols])

    down_half(0, done[n_chunks - LAG])

    y = yconv_ref[...] + cb_ref[...]
    y = _layer_norm(y, cg_ref[...], cbb_ref[...])
    y = _silu(y)
    ycat_ref[:, 0:d_sgu] = ya_ref[...]
    ycat_ref[:, d_sgu:d_sgu + d_conv] = y.astype(jnp.bfloat16)
    r1 = alpha * x_ref[...] + _dot(ycat_ref[...], wout_ref[...])
    outs = []
    for i, r0 in enumerate(range(0, tm, LN_ROWS)):
        xi = r1[r0:r0 + LN_ROWS, :]
        if i >= LN_DEPTH:
            xi = jnp.concatenate([_after(xi[:, 0:LANES], outs[i - LN_DEPTH]), xi[:, LANES:]], axis=1)
        outs.append(_layer_norm(xi, g1_ref[...], b1_ref[...]))
        x1_ref[r0:r0 + LN_ROWS, :] = outs[-1]

    for r0 in range(0, tm, tm // 2):
        down_half(1, r1, r0, tm // 2)
        rows = slice(r0, r0 + tm // 2)
        o_ref[rows, :] = _layer_norm(res_ref[rows, :], g2_ref[...], b2_ref[...])


def _const_spec(shape):
    return pl.BlockSpec(shape, lambda i: (0,) * len(shape), pipeline_mode=pl.Buffered(1))


def _layer(x2d, seq, w_in, sgu_ln_g, sgu_ln_b, w_s, b_s, conv_w, conv_b, conv_ln_g, conv_ln_b,
           w_out, ln1_g, ln1_b, w_gate, w_up, w_down, ln2_g, ln2_b, alpha):
    n_tok, d_model = x2d.shape
    d_sgu = sgu_ln_g.shape[0]
    d_conv = conv_ln_g.shape[0]
    n_heads, chunk, _ = w_s.shape
    conv_width = conv_w.shape[0]
    d_ff = w_gate.shape[1]
    tm = TM
    assert chunk == CHUNK and n_heads * HEAD_DIM == d_sgu and d_sgu % LANES == 0
    assert conv_width - 1 <= HALO and d_ff % FF_CHUNK == 0 and d_model % MXU_COLS == 0
    assert seq % tm == 0 and tm % CHUNK == 0 and tm % CONV_ROWS == 0
    assert seq % CHUNK == 0 and TM_FRONT % CHUNK == 0 and n_tok % TM_FRONT == 0
    n_tiles = n_tok // tm

    bf16 = jnp.bfloat16
    row = lambda p: p.reshape(1, -1)
    ws_pairs = w_s.reshape(n_heads // 2, 2, chunk, chunk).transpose(0, 2, 1, 3).reshape(
        n_heads // 2, chunk, 2 * chunk).astype(bf16)
    bias = jnp.repeat(b_s.T, HEAD_DIM, axis=1)
    params = pltpu.CompilerParams(dimension_semantics=("arbitrary",), vmem_limit_bytes=VMEM_LIMIT_BYTES)

    tile = lambda width: pl.BlockSpec((TM_FRONT, width), lambda i: (i, 0))
    h, ya = pl.pallas_call(
        _front_kernel,
        grid=(n_tok // TM_FRONT,),
        in_specs=[
            tile(d_model),
            _const_spec((d_model, 2 * d_sgu + 2 * d_conv)),
            _const_spec((1, d_sgu)), _const_spec((1, d_sgu)),
            _const_spec((n_heads // 2, chunk, 2 * chunk)),
            _const_spec((chunk, d_sgu)),
        ],
        out_specs=[tile(d_conv), tile(d_sgu)],
        out_shape=[jax.ShapeDtypeStruct((n_tok, d_conv), jnp.float32),
                   jax.ShapeDtypeStruct((n_tok, d_sgu), bf16)],
        compiler_params=params,
        name="front",
    )(x2d, w_in.astype(bf16), row(sgu_ln_g), row(sgu_ln_b), ws_pairs, bias)

    cur = lambda width: pl.BlockSpec((tm, width), lambda i: (jnp.minimum(i, n_tiles - 1), 0))
    prev = pl.BlockSpec((tm, d_model), lambda i: (jnp.maximum(i - 1, 0), 0))
    return pl.pallas_call(
        functools.partial(_back_kernel, tiles_per_seq=seq // tm, alpha=alpha),
        grid=(n_tiles + 1,),
        in_specs=[
            cur(d_conv), cur(d_sgu), cur(d_model),
            _const_spec((conv_width, d_conv)),
            _const_spec((1, d_conv)), _const_spec((1, d_conv)), _const_spec((1, d_conv)),
            _const_spec((d_sgu + d_conv, d_model)),
            _const_spec((1, d_model)), _const_spec((1, d_model)),
            _const_spec((d_model, d_ff)), _const_spec((d_model, d_ff)), _const_spec((d_ff, d_model)),
            _const_spec((1, d_model)), _const_spec((1, d_model)),
        ],
        out_specs=prev,
        out_shape=jax.ShapeDtypeStruct((n_tok, d_model), jnp.float32),
        scratch_shapes=[
            pltpu.VMEM((tm, d_model), jnp.float32),
            pltpu.VMEM((HALO + tm, d_conv), jnp.float32),
            pltpu.VMEM((tm, d_conv), jnp.float32),
            pltpu.VMEM((SUBLANES - 1, HALO + tm, LANES), jnp.float32),
            pltpu.VMEM((tm, d_sgu + d_conv), bf16),
            pltpu.VMEM((tm, d_ff), bf16),
            pltpu.VMEM((tm, d_model), jnp.float32),
        ],
        compiler_params=params,
        name="back",
    )(h, ya, x2d, conv_w, row(conv_b), row(conv_ln_g), row(conv_ln_b), w_out.astype(bf16),
      row(ln1_g), row(ln1_b), w_gate.astype(bf16), w_up.astype(bf16), w_down.astype(bf16),
      row(ln2_g), row(ln2_b))


def kernel(x, w_in, sgu_ln_g, sgu_ln_b, w_s, b_s, conv_w, conv_b, conv_ln_g, conv_ln_b, w_out, ln1_g, ln1_b, w_gate, w_up, w_down, ln2_g, ln2_b):
    batch, seq, d_model = x.shape
    depth = w_in.shape[0]
    alpha = (2.0 * depth) ** 0.25
    x2d = x.reshape(batch * seq, d_model)
    for l in range(depth):
        x2d = _layer(x2d, seq, w_in[l], sgu_ln_g[l], sgu_ln_b[l], w_s[l], b_s[l], conv_w[l], conv_b[l],
                     conv_ln_g[l], conv_ln_b[l], w_out[l], ln1_g[l], ln1_b[l],
                     w_gate[l], w_up[l], w_down[l], ln2_g[l], ln2_b[l], alpha)
    return x2d.reshape(batch, seq, d_model)
```

```python
import functools

import jax
import jax.numpy as jnp
from jax import lax
from jax.experimental import pallas as pl
from jax.experimental.pallas import tpu as pltpu

HEAD_DIM = 64
CHUNK = 128
LN_EPS = 1e-5
LANES = 128
SUBLANES = 8
MXU_COLS = 256
HALO = 32
CONV_ROWS = 32
ACC_ROWS = 16
FF_CHUNK = MXU_COLS
GATE_ROWS = 16
LN_ROWS = 8
LN_DEPTH = 8
UNIT_FREE_CHUNKS = 0
OUT_CHUNK = 5
LAG = 3
TM = 512
TM_FRONT = 1024
VMEM_LIMIT_BYTES = 56 * 1024 * 1024


def _layer_norm(x, g, b):
    mu = jnp.mean(x, axis=-1, keepdims=True)
    xc = x - mu
    var = jnp.mean(xc * xc, axis=-1, keepdims=True)
    return xc * lax.rsqrt(var + LN_EPS) * g + b


def _gelu(x):
    return 0.5 * x * (1.0 + lax.erf(x * (0.5 ** 0.5)))


def _silu(x):
    hx = 0.5 * x
    return hx + hx * jnp.tanh(hx)


def _dot(a, b):
    return jnp.dot(a, b, preferred_element_type=jnp.float32)


def _zero_bits(dep):
    bits = pltpu.bitcast(dep[0:SUBLANES, 0:LANES], jnp.uint32)
    return lax.shift_right_logical(lax.shift_right_logical(bits, jnp.uint32(16)), jnp.uint32(16))


def _after(x, dep):
    zero = jnp.tile(_zero_bits(dep), (x.shape[0] // SUBLANES, x.shape[1] // LANES))
    return jnp.where(zero == 0, x, 0.0)


def _front_kernel(x_ref, win_ref, sg_ref, sb_ref, ws_ref, bs_ref, h_ref, ya_ref):
    tm = x_ref.shape[0]
    d_sgu = sg_ref.shape[1]
    d_conv = h_ref.shape[1]
    n_chunks = tm // CHUNK
    xb = x_ref[...].astype(jnp.bfloat16)

    a = _dot(xb, win_ref[:, 2 * d_sgu:2 * d_sgu + d_conv])
    g = _dot(xb, win_ref[:, 2 * d_sgu + d_conv:2 * d_sgu + 2 * d_conv])
    h_ref[...] = a * jax.nn.sigmoid(g)

    u = _gelu(_dot(xb, win_ref[:, 0:d_sgu]))
    v = _gelu(_dot(xb, win_ref[:, d_sgu:2 * d_sgu]))
    v = _layer_norm(v, sg_ref[...], sb_ref[...]).astype(jnp.bfloat16)

    row = lax.broadcasted_iota(jnp.int32, (CHUNK, 2 * CHUNK), 0)
    col = lax.broadcasted_iota(jnp.int32, (CHUNK, 2 * CHUNK), 1)
    causal = (col % CHUNK) <= row
    lane = lax.broadcasted_iota(jnp.int32, (CHUNK, LANES), 1)
    low_head = lane < HEAD_DIM
    zero = jnp.zeros((CHUNK, LANES), jnp.bfloat16)
    for j in range(d_sgu // LANES):
        w_pair = jnp.where(causal, ws_ref[j], jnp.zeros_like(ws_ref[j]))
        bias = bs_ref[:, j * LANES:(j + 1) * LANES]
        for c0 in range(0, n_chunks, 2):
            cs = list(range(c0, min(c0 + 2, n_chunks)))
            rhs = []
            for c in cs:
                vj = v[c * CHUNK:(c + 1) * CHUNK, j * LANES:(j + 1) * LANES]
                rhs.append(jnp.concatenate(
                    [jnp.where(low_head, vj, zero), jnp.where(low_head, zero, vj)], axis=0))
            mixed = _dot(w_pair, jnp.concatenate(rhs, axis=1) if len(rhs) > 1 else rhs[0])
            for i, c in enumerate(cs):
                uc = u[c * CHUNK:(c + 1) * CHUNK, j * LANES:(j + 1) * LANES]
                out_a = uc * (mixed[:, i * LANES:(i + 1) * LANES] + bias)
                ya_ref[c * CHUNK:(c + 1) * CHUNK, j * LANES:(j + 1) * LANES] = out_a.astype(jnp.bfloat16)


def _conv_unit(unit, hext_ref, cw_ref, yconv_ref, rot_ref, after):
    conv_width = cw_ref.shape[0]
    cb, rb = divmod(unit, yconv_ref.shape[0] // CONV_ROWS)
    t0 = rb * CONV_ROWS
    lanes = slice(cb * LANES, (cb + 1) * LANES)
    base = HALO - (conv_width - 1)
    reach = max(o - o % SUBLANES for o in range(base, base + conv_width) if o % SUBLANES)
    lo, hi = (0 if rb == 0 else t0 + reach), t0 + CONV_ROWS + reach
    src = _after(hext_ref[lo:hi + SUBLANES, lanes], after)
    for r in range(1, SUBLANES):
        rot_ref[r - 1, lo:hi, :] = pltpu.roll(src, hi - lo + SUBLANES - r, 0)[0:hi - lo, :]
    acc = None
    for i0 in range(0, CONV_ROWS, ACC_ROWS):
        prev, acc = acc, None
        for o in range(base, base + conv_width):
            r = o % SUBLANES
            start = t0 + i0 + o - r
            if r == 0:
                win = hext_ref[start:start + ACC_ROWS, lanes]
            else:
                win = rot_ref[r - 1, start:start + ACC_ROWS, :]
            term = win * cw_ref[o - base:o - base + 1, lanes]
            if acc is None:
                acc = term if prev is None else _after(term, prev)
            else:
                acc = acc + term
        yconv_ref[t0 + i0:t0 + i0 + ACC_ROWS, lanes] = acc
    return acc


def _paced_layer_norm(x, g_ref, b_ref, o_ref):
    outs = []
    for i, r0 in enumerate(range(0, x.shape[0], LN_ROWS)):
        xi = x[r0:r0 + LN_ROWS, :]
        if i >= LN_DEPTH:
            xi = jnp.concatenate([_after(xi[:, 0:LANES], outs[i - LN_DEPTH]), xi[:, LANES:]], axis=1)
        outs.append(_layer_norm(xi, g_ref[...], b_ref[...]))
        o_ref[r0:r0 + LN_ROWS, :] = outs[-1]
    token = outs[-1][:, 0:LANES]
    for out in outs[-LN_DEPTH:-1]:
        token = _after(token, out)
    return token


def _gated_lhs(lhs, head_f32, token):
    head = _after(head_f32, token).astype(jnp.bfloat16)
    rows = head.shape[0]
    return jnp.concatenate(
        [jnp.concatenate([head, lhs[rows:, 0:MXU_COLS]], axis=0), lhs[:, MXU_COLS:]], axis=1)


def _back_kernel(h_ref, ya_ref, x_ref, cw_ref, cb_ref, cg_ref, cbb_ref, wout_ref, g1_ref, b1_ref,
                 wg_ref, wu_ref, wd_ref, g2_ref, b2_ref, o_ref,
                 x1_ref, hext_ref, yconv_ref, rot_ref, ycat_ref, hid_ref, res_ref, *, tiles_per_seq, alpha):
    tm, d_model = x_ref.shape
    d_conv = h_ref.shape[1]
    d_sgu = ya_ref.shape[1]
    d_ff = wg_ref.shape[1]
    n_chunks = d_ff // FF_CHUNK
    n_units = (tm // CONV_ROWS) * (d_conv // LANES)
    step = pl.program_id(0)

    @pl.when(step == 0)
    def _():
        x1_ref[...] = jnp.zeros_like(x1_ref)
        res_ref[...] = jnp.zeros_like(res_ref)
        hext_ref[tm:tm + HALO, :] = jnp.zeros((HALO, d_conv), jnp.float32)

    out_done = _paced_layer_norm(res_ref[...], g2_ref, b2_ref, o_ref)

    seq_start = (step % tiles_per_seq) == 0
    hext_ref[0:HALO, :] = jnp.where(seq_start, 0.0, hext_ref[tm:tm + HALO, :])
    hext_ref[HALO:HALO + tm, :] = h_ref[...]

    xfb = x1_ref[...].astype(jnp.bfloat16)
    res_ref[...] = alpha * x1_ref[...]
    unit = 0
    done = []
    for ci in range(n_chunks):
        cols = slice(ci * FF_CHUNK, (ci + 1) * FF_CHUNK)
        if ci < LAG:
            lhs = xfb
        else:
            token = done[ci - LAG] if ci != OUT_CHUNK else _after(out_done, done[ci - LAG])
            lhs = _gated_lhs(xfb, x1_ref[0:GATE_ROWS, 0:MXU_COLS], token)
        gate = _dot(lhs, wg_ref[:, cols])
        up = _dot(lhs, wu_ref[:, cols])
        hid_ref[:, cols] = (_silu(gate) * up).astype(jnp.bfloat16)
        token = gate
        for _ in range(min(n_units, -(-(ci + 1) * n_units // (n_chunks - UNIT_FREE_CHUNKS))) - unit):
            token = _conv_unit(unit, hext_ref, cw_ref, yconv_ref, rot_ref, token)
            unit += 1
        done.append(token)

    def down_half(half, token):
        lhs = _gated_lhs(hid_ref[...], hid_ref[0:GATE_ROWS, 0:MXU_COLS].astype(jnp.float32), token)
        for n in range(2 * half, 2 * half + 2):
            cols = slice(n * (d_model // 4), (n + 1) * (d_model // 4))
            res_ref[:, cols] += _dot(lhs, wd_ref[:, cols])

    down_half(0, done[n_chunks - LAG])

    y = yconv_ref[...] + cb_ref[...]
    y = _layer_norm(y, cg_ref[...], cbb_ref[...])
    y = _silu(y)
    ycat_ref[:, 0:d_sgu] = ya_ref[...]
    ycat_ref[:, d_sgu:d_sgu + d_conv] = y.astype(jnp.bfloat16)
    r1 = alpha * x_ref[...] + _dot(ycat_ref[...], wout_ref[...])
    _paced_layer_norm(r1, g1_ref, b1_ref, x1_ref)

    down_half(1, r1)


def _const_spec(shape):
    return pl.BlockSpec(shape, lambda i: (0,) * len(shape), pipeline_mode=pl.Buffered(1))


def _layer(x2d, seq, w_in, sgu_ln_g, sgu_ln_b, w_s, b_s, conv_w, conv_b, conv_ln_g, conv_ln_b,
           w_out, ln1_g, ln1_b, w_gate, w_up, w_down, ln2_g, ln2_b, alpha):
    n_tok, d_model = x2d.shape
    d_sgu = sgu_ln_g.shape[0]
    d_conv = conv_ln_g.shape[0]
    n_heads, chunk, _ = w_s.shape
    conv_width = conv_w.shape[0]
    d_ff = w_gate.shape[1]
    tm = TM
    assert chunk == CHUNK and n_heads * HEAD_DIM == d_sgu and d_sgu % LANES == 0
    assert conv_width - 1 <= HALO and d_ff % FF_CHUNK == 0 and d_model % MXU_COLS == 0
    assert seq % tm == 0 and tm % CHUNK == 0 and tm % CONV_ROWS == 0
    assert seq % CHUNK == 0 and TM_FRONT % CHUNK == 0 and n_tok % TM_FRONT == 0
    n_tiles = n_tok // tm

    bf16 = jnp.bfloat16
    row = lambda p: p.reshape(1, -1)
    ws_pairs = w_s.reshape(n_heads // 2, 2, chunk, chunk).transpose(0, 2, 1, 3).reshape(
        n_heads // 2, chunk, 2 * chunk).astype(bf16)
    bias = jnp.repeat(b_s.T, HEAD_DIM, axis=1)
    params = pltpu.CompilerParams(dimension_semantics=("arbitrary",), vmem_limit_bytes=VMEM_LIMIT_BYTES)

    tile = lambda width: pl.BlockSpec((TM_FRONT, width), lambda i: (i, 0))
    h, ya = pl.pallas_call(
        _front_kernel,
        grid=(n_tok // TM_FRONT,),
        in_specs=[
            tile(d_model),
            _const_spec((d_model, 2 * d_sgu + 2 * d_conv)),
            _const_spec((1, d_sgu)), _const_spec((1, d_sgu)),
            _const_spec((n_heads // 2, chunk, 2 * chunk)),
            _const_spec((chunk, d_sgu)),
        ],
        out_specs=[tile(d_conv), tile(d_sgu)],
        out_shape=[jax.ShapeDtypeStruct((n_tok, d_conv), jnp.float32),
                   jax.ShapeDtypeStruct((n_tok, d_sgu), bf16)],
        compiler_params=params,
        name="front",
    )(x2d, w_in.astype(bf16), row(sgu_ln_g), row(sgu_ln_b), ws_pairs, bias)

    cur = lambda width: pl.BlockSpec((tm, width), lambda i: (jnp.minimum(i, n_tiles - 1), 0))
    prev = pl.BlockSpec((tm, d_model), lambda i: (jnp.maximum(i - 2, 0), 0))
    return pl.pallas_call(
        functools.partial(_back_kernel, tiles_per_seq=seq // tm, alpha=alpha),
        grid=(n_tiles + 2,),
        in_specs=[
            cur(d_conv), cur(d_sgu), cur(d_model),
            _const_spec((conv_width, d_conv)),
            _const_spec((1, d_conv)), _const_spec((1, d_conv)), _const_spec((1, d_conv)),
            _const_spec((d_sgu + d_conv, d_model)),
            _const_spec((1, d_model)), _const_spec((1, d_model)),
            _const_spec((d_model, d_ff)), _const_spec((d_model, d_ff)), _const_spec((d_ff, d_model)),
            _const_spec((1, d_model)), _const_spec((1, d_model)),
        ],
        out_specs=prev,
        out_shape=jax.ShapeDtypeStruct((n_tok, d_model), jnp.float32),
        scratch_shapes=[
            pltpu.VMEM((tm, d_model), jnp.float32),
            pltpu.VMEM((HALO + tm, d_conv), jnp.float32),
            pltpu.VMEM((tm, d_conv), jnp.float32),
            pltpu.VMEM((SUBLANES - 1, HALO + tm, LANES), jnp.float32),
            pltpu.VMEM((tm, d_sgu + d_conv), bf16),
            pltpu.VMEM((tm, d_ff), bf16),
            pltpu.VMEM((tm, d_model), jnp.float32),
        ],
        compiler_params=params,
        name="back",
    )(h, ya, x2d, conv_w, row(conv_b), row(conv_ln_g), row(conv_ln_b), w_out.astype(bf16),
      row(ln1_g), row(ln1_b), w_gate.astype(bf16), w_up.astype(bf16), w_down.astype(bf16),
      row(ln2_g), row(ln2_b))


def kernel(x, w_in, sgu_ln_g, sgu_ln_b, w_s, b_s, conv_w, conv_b, conv_ln_g, conv_ln_b, w_out, ln1_g, ln1_b, w_gate, w_up, w_down, ln2_g, ln2_b):
    batch, seq, d_model = x.shape
    depth = w_in.shape[0]
    alpha = (2.0 * depth) ** 0.25
    x2d = x.reshape(batch * seq, d_model)
    for l in range(depth):
        x2d = _layer(x2d, seq, w_in[l], sgu_ln_g[l], sgu_ln_b[l], w_s[l], b_s[l], conv_w[l], conv_b[l],
                     conv_ln_g[l], conv_ln_b[l], w_out[l], ln1_g[l], ln1_b[l],
                     w_gate[l], w_up[l], w_down[l], ln2_g[l], ln2_b[l], alpha)
    return x2d.reshape(batch, seq, d_model)
```

```python
import functools

import jax
import jax.numpy as jnp
from jax import lax
from jax.experimental import pallas as pl
from jax.experimental.pallas import tpu as pltpu

HEAD_DIM = 64
CHUNK = 128
LN_EPS = 1e-5
LANES = 128
SUBLANES = 8
MXU_COLS = 256
HALO = 32
CONV_ROWS = 32
ACC_ROWS = 16
FF_CHUNK = MXU_COLS
GATE_ROWS = 16
LN_ROWS = 8
LN_DEPTH = 8
UNIT_FREE_CHUNKS = 0
OUT_CHUNK = 5
LAG = 3
TM = 512
TM_FRONT = 1024
VMEM_LIMIT_BYTES = 56 * 1024 * 1024


def _layer_norm(x, g, b):
    mu = jnp.mean(x, axis=-1, keepdims=True)
    xc = x - mu
    var = jnp.mean(xc * xc, axis=-1, keepdims=True)
    return xc * lax.rsqrt(var + LN_EPS) * g + b


def _gelu(x):
    return 0.5 * x * (1.0 + lax.erf(x * (0.5 ** 0.5)))


def _silu(x):
    hx = 0.5 * x
    return hx + hx * jnp.tanh(hx)


def _dot(a, b):
    return jnp.dot(a, b, preferred_element_type=jnp.float32)


def _zero_bits(dep):
    bits = pltpu.bitcast(dep[0:SUBLANES, 0:LANES], jnp.uint32)
    return lax.shift_right_logical(lax.shift_right_logical(bits, jnp.uint32(16)), jnp.uint32(16))


def _after(x, dep):
    zero = jnp.tile(_zero_bits(dep), (x.shape[0] // SUBLANES, x.shape[1] // LANES))
    return jnp.where(zero == 0, x, 0.0)


def _front_kernel(x_ref, win_ref, sg_ref, sb_ref, ws_ref, bs_ref, h_ref, ya_ref):
    tm = x_ref.shape[0]
    d_sgu = sg_ref.shape[1]
    d_conv = h_ref.shape[1]
    n_chunks = tm // CHUNK
    xb = x_ref[...].astype(jnp.bfloat16)

    u = _dot(xb, win_ref[:, 0:d_sgu])
    v = _dot(xb, win_ref[:, d_sgu:2 * d_sgu])

    lhs = _gated_lhs(xb, x_ref[0:GATE_ROWS, 0:MXU_COLS], v)
    a = _dot(lhs, win_ref[:, 2 * d_sgu:2 * d_sgu + d_conv])
    g = _dot(lhs, win_ref[:, 2 * d_sgu + d_conv:2 * d_sgu + 2 * d_conv])
    h_ref[...] = a * jax.nn.sigmoid(g)

    u = _gelu(u)
    v = _layer_norm(_gelu(v), sg_ref[...], sb_ref[...]).astype(jnp.bfloat16)

    row = lax.broadcasted_iota(jnp.int32, (CHUNK, 2 * CHUNK), 0)
    col = lax.broadcasted_iota(jnp.int32, (CHUNK, 2 * CHUNK), 1)
    causal = (col % CHUNK) <= row
    lane = lax.broadcasted_iota(jnp.int32, (CHUNK, LANES), 1)
    low_head = lane < HEAD_DIM
    zero = jnp.zeros((CHUNK, LANES), jnp.bfloat16)
    for j in range(d_sgu // LANES):
        w_pair = jnp.where(causal, ws_ref[j], jnp.zeros_like(ws_ref[j]))
        bias = bs_ref[:, j * LANES:(j + 1) * LANES]
        for c0 in range(0, n_chunks, 2):
            cs = list(range(c0, min(c0 + 2, n_chunks)))
            rhs = []
            for c in cs:
                vj = v[c * CHUNK:(c + 1) * CHUNK, j * LANES:(j + 1) * LANES]
                rhs.append(jnp.concatenate(
                    [jnp.where(low_head, vj, zero), jnp.where(low_head, zero, vj)], axis=0))
            mixed = _dot(w_pair, jnp.concatenate(rhs, axis=1) if len(rhs) > 1 else rhs[0])
            for i, c in enumerate(cs):
                uc = u[c * CHUNK:(c + 1) * CHUNK, j * LANES:(j + 1) * LANES]
                out_a = uc * (mixed[:, i * LANES:(i + 1) * LANES] + bias)
                ya_ref[c * CHUNK:(c + 1) * CHUNK, j * LANES:(j + 1) * LANES] = out_a.astype(jnp.bfloat16)


def _conv_unit(unit, hext_ref, cw_ref, yconv_ref, rot_ref, after):
    conv_width = cw_ref.shape[0]
    cb, rb = divmod(unit, yconv_ref.shape[0] // CONV_ROWS)
    t0 = rb * CONV_ROWS
    lanes = slice(cb * LANES, (cb + 1) * LANES)
    base = HALO - (conv_width - 1)
    reach = max(o - o % SUBLANES for o in range(base, base + conv_width) if o % SUBLANES)
    lo, hi = (0 if rb == 0 else t0 + reach), t0 + CONV_ROWS + reach
    src = _after(hext_ref[lo:hi + SUBLANES, lanes], after)
    for r in range(1, SUBLANES):
        rot_ref[r - 1, lo:hi, :] = pltpu.roll(src, hi - lo + SUBLANES - r, 0)[0:hi - lo, :]
    acc = None
    for i0 in range(0, CONV_ROWS, ACC_ROWS):
        prev, acc = acc, None
        for o in range(base, base + conv_width):
            r = o % SUBLANES
            start = t0 + i0 + o - r
            if r == 0:
                win = hext_ref[start:start + ACC_ROWS, lanes]
            else:
                win = rot_ref[r - 1, start:start + ACC_ROWS, :]
            term = win * cw_ref[o - base:o - base + 1, lanes]
            if acc is None:
                acc = term if prev is None else _after(term, prev)
            else:
                acc = acc + term
        yconv_ref[t0 + i0:t0 + i0 + ACC_ROWS, lanes] = acc
    return acc


def _paced_layer_norm(x, g_ref, b_ref, o_ref):
    outs = []
    for i, r0 in enumerate(range(0, x.shape[0], LN_ROWS)):
        xi = x[r0:r0 + LN_ROWS, :]
        if i >= LN_DEPTH:
            xi = jnp.concatenate([_after(xi[:, 0:LANES], outs[i - LN_DEPTH]), xi[:, LANES:]], axis=1)
        outs.append(_layer_norm(xi, g_ref[...], b_ref[...]))
        o_ref[r0:r0 + LN_ROWS, :] = outs[-1]
    token = outs[-1][:, 0:LANES]
    for out in outs[-LN_DEPTH:-1]:
        token = _after(token, out)
    return token


def _gated_lhs(lhs, head_f32, token):
    head = _after(head_f32, token).astype(jnp.bfloat16)
    rows = head.shape[0]
    return jnp.concatenate(
        [jnp.concatenate([head, lhs[rows:, 0:MXU_COLS]], axis=0), lhs[:, MXU_COLS:]], axis=1)


def _back_kernel(h_ref, ya_ref, x_ref, cw_ref, cb_ref, cg_ref, cbb_ref, wout_ref, g1_ref, b1_ref,
                 wg_ref, wu_ref, wd_ref, g2_ref, b2_ref, o_ref,
                 x1_ref, hext_ref, yconv_ref, rot_ref, ycat_ref, hid_ref, res_ref, *, tiles_per_seq, alpha):
    tm, d_model = x_ref.shape
    d_conv = h_ref.shape[1]
    d_sgu = ya_ref.shape[1]
    d_ff = wg_ref.shape[1]
    n_chunks = d_ff // FF_CHUNK
    n_units = (tm // CONV_ROWS) * (d_conv // LANES)
    step = pl.program_id(0)

    @pl.when(step == 0)
    def _():
        x1_ref[...] = jnp.zeros_like(x1_ref)
        res_ref[...] = jnp.zeros_like(res_ref)
        hext_ref[tm:tm + HALO, :] = jnp.zeros((HALO, d_conv), jnp.float32)

    out_done = _paced_layer_norm(res_ref[...], g2_ref, b2_ref, o_ref)

    seq_start = (step % tiles_per_seq) == 0
    hext_ref[0:HALO, :] = jnp.where(seq_start, 0.0, hext_ref[tm:tm + HALO, :])
    hext_ref[HALO:HALO + tm, :] = h_ref[...]

    xfb = x1_ref[...].astype(jnp.bfloat16)
    res_ref[...] = alpha * x1_ref[...]
    unit = 0
    done = []
    for ci in range(n_chunks):
        cols = slice(ci * FF_CHUNK, (ci + 1) * FF_CHUNK)
        if ci < LAG:
            lhs = xfb
        else:
            token = done[ci - LAG] if ci != OUT_CHUNK else _after(out_done, done[ci - LAG])
            lhs = _gated_lhs(xfb, x1_ref[0:GATE_ROWS, 0:MXU_COLS], token)
        gate = _dot(lhs, wg_ref[:, cols])
        up = _dot(lhs, wu_ref[:, cols])
        hid_ref[:, cols] = (_silu(gate) * up).astype(jnp.bfloat16)
        token = gate
        for _ in range(min(n_units, -(-(ci + 1) * n_units // (n_chunks - UNIT_FREE_CHUNKS))) - unit):
            token = _conv_unit(unit, hext_ref, cw_ref, yconv_ref, rot_ref, token)
            unit += 1
        done.append(token)

    def down_half(half, token):
        lhs = _gated_lhs(hid_ref[...], hid_ref[0:GATE_ROWS, 0:MXU_COLS].astype(jnp.float32), token)
        for n in range(2 * half, 2 * half + 2):
            cols = slice(n * (d_model // 4), (n + 1) * (d_model // 4))
            res_ref[:, cols] += _dot(lhs, wd_ref[:, cols])

    down_half(0, done[n_chunks - LAG])

    y = yconv_ref[...] + cb_ref[...]
    y = _layer_norm(y, cg_ref[...], cbb_ref[...])
    y = _silu(y)
    ycat_ref[:, 0:d_sgu] = ya_ref[...]
    ycat_ref[:, d_sgu:d_sgu + d_conv] = y.astype(jnp.bfloat16)
    r1 = alpha * x_ref[...] + _dot(ycat_ref[...], wout_ref[...])
    _paced_layer_norm(r1, g1_ref, b1_ref, x1_ref)

    down_half(1, r1)


def _const_spec(shape):
    return pl.BlockSpec(shape, lambda i: (0,) * len(shape), pipeline_mode=pl.Buffered(1))


def _layer(x2d, seq, w_in, sgu_ln_g, sgu_ln_b, w_s, b_s, conv_w, conv_b, conv_ln_g, conv_ln_b,
           w_out, ln1_g, ln1_b, w_gate, w_up, w_down, ln2_g, ln2_b, alpha):
    n_tok, d_model = x2d.shape
    d_sgu = sgu_ln_g.shape[0]
    d_conv = conv_ln_g.shape[0]
    n_heads, chunk, _ = w_s.shape
    conv_width = conv_w.shape[0]
    d_ff = w_gate.shape[1]
    tm = TM
    assert chunk == CHUNK and n_heads * HEAD_DIM == d_sgu and d_sgu % LANES == 0
    assert conv_width - 1 <= HALO and d_ff % FF_CHUNK == 0 and d_model % MXU_COLS == 0
    assert seq % tm == 0 and tm % CHUNK == 0 and tm % CONV_ROWS == 0
    assert seq % CHUNK == 0 and TM_FRONT % CHUNK == 0 and n_tok % TM_FRONT == 0
    n_tiles = n_tok // tm

    bf16 = jnp.bfloat16
    row = lambda p: p.reshape(1, -1)
    ws_pairs = w_s.reshape(n_heads // 2, 2, chunk, chunk).transpose(0, 2, 1, 3).reshape(
        n_heads // 2, chunk, 2 * chunk).astype(bf16)
    bias = jnp.repeat(b_s.T, HEAD_DIM, axis=1)
    params = pltpu.CompilerParams(dimension_semantics=("arbitrary",), vmem_limit_bytes=VMEM_LIMIT_BYTES)

    tile = lambda width: pl.BlockSpec((TM_FRONT, width), lambda i: (i, 0))
    h, ya = pl.pallas_call(
        _front_kernel,
        grid=(n_tok // TM_FRONT,),
        in_specs=[
            tile(d_model),
            _const_spec((d_model, 2 * d_sgu + 2 * d_conv)),
            _const_spec((1, d_sgu)), _const_spec((1, d_sgu)),
            _const_spec((n_heads // 2, chunk, 2 * chunk)),
            _const_spec((chunk, d_sgu)),
        ],
        out_specs=[tile(d_conv), tile(d_sgu)],
        out_shape=[jax.ShapeDtypeStruct((n_tok, d_conv), jnp.float32),
                   jax.ShapeDtypeStruct((n_tok, d_sgu), bf16)],
        compiler_params=params,
        name="front",
    )(x2d, w_in.astype(bf16), row(sgu_ln_g), row(sgu_ln_b), ws_pairs, bias)

    cur = lambda width: pl.BlockSpec((tm, width), lambda i: (jnp.minimum(i, n_tiles - 1), 0))
    prev = pl.BlockSpec((tm, d_model), lambda i: (jnp.maximum(i - 2, 0), 0))
    return pl.pallas_call(
        functools.partial(_back_kernel, tiles_per_seq=seq // tm, alpha=alpha),
        grid=(n_tiles + 2,),
        in_specs=[
            cur(d_conv), cur(d_sgu), cur(d_model),
            _const_spec((conv_width, d_conv)),
            _const_spec((1, d_conv)), _const_spec((1, d_conv)), _const_spec((1, d_conv)),
            _const_spec((d_sgu + d_conv, d_model)),
            _const_spec((1, d_model)), _const_spec((1, d_model)),
            _const_spec((d_model, d_ff)), _const_spec((d_model, d_ff)), _const_spec((d_ff, d_model)),
            _const_spec((1, d_model)), _const_spec((1, d_model)),
        ],
        out_specs=prev,
        out_shape=jax.ShapeDtypeStruct((n_tok, d_model), jnp.float32),
        scratch_shapes=[
            pltpu.VMEM((tm, d_model), jnp.float32),
            pltpu.VMEM((HALO + tm, d_conv), jnp.float32),
            pltpu.VMEM((tm, d_conv), jnp.float32),
            pltpu.VMEM((SUBLANES - 1, HALO + tm, LANES), jnp.float32),
            pltpu.VMEM((tm, d_sgu + d_conv), bf16),
            pltpu.VMEM((tm, d_ff), bf16),
            pltpu.VMEM((tm, d_model), jnp.float32),
        ],
        compiler_params=params,
        name="back",
    )(h, ya, x2d, conv_w, row(conv_b), row(conv_ln_g), row(conv_ln_b), w_out.astype(bf16),
      row(ln1_g), row(ln1_b), w_gate.astype(bf16), w_up.astype(bf16), w_down.astype(bf16),
      row(ln2_g), row(ln2_b))


def kernel(x, w_in, sgu_ln_g, sgu_ln_b, w_s, b_s, conv_w, conv_b, conv_ln_g, conv_ln_b, w_out, ln1_g, ln1_b, w_gate, w_up, w_down, ln2_g, ln2_b):
    batch, seq, d_model = x.shape
    depth = w_in.shape[0]
    alpha = (2.0 * depth) ** 0.25
    x2d = x.reshape(batch * seq, d_model)
    for l in range(depth):
        x2d = _layer(x2d, seq, w_in[l], sgu_ln_g[l], sgu_ln_b[l], w_s[l], b_s[l], conv_w[l], conv_b[l],
                     conv_ln_g[l], conv_ln_b[l], w_out[l], ln1_g[l], ln1_b[l],
                     w_gate[l], w_up[l], w_down[l], ln2_g[l], ln2_b[l], alpha)
    return x2d.reshape(batch, seq, d_model)
```

```python
import functools

import jax
import jax.numpy as jnp
from jax import lax
from jax.experimental import pallas as pl
from jax.experimental.pallas import tpu as pltpu

HEAD_DIM = 64
CHUNK = 128
LN_EPS = 1e-5
LANES = 128
SUBLANES = 8
MXU_COLS = 256
HALO = 32
CONV_ROWS = 32
ACC_ROWS = 16
FF_CHUNK = MXU_COLS
GATE_ROWS = 16
LN_ROWS = 8
LN_DEPTH = 8
OUT_CHUNK = 5
LAG = 3
TM = 512
TM_FRONT = 1024
VMEM_LIMIT_BYTES = 56 * 1024 * 1024


def _layer_norm(x, g, b):
    mu = jnp.mean(x, axis=-1, keepdims=True)
    xc = x - mu
    var = jnp.mean(xc * xc, axis=-1, keepdims=True)
    return xc * lax.rsqrt(var + LN_EPS) * g + b


def _gelu(x):
    return 0.5 * x * (1.0 + lax.erf(x * (0.5 ** 0.5)))


def _silu(x):
    hx = 0.5 * x
    return hx + hx * jnp.tanh(hx)


def _dot(a, b):
    return jnp.dot(a, b, preferred_element_type=jnp.float32)


def _zero_bits(dep):
    bits = pltpu.bitcast(dep[0:SUBLANES, 0:LANES], jnp.uint32)
    return lax.shift_right_logical(lax.shift_right_logical(bits, jnp.uint32(16)), jnp.uint32(16))


def _after(x, dep):
    zero = jnp.tile(_zero_bits(dep), (x.shape[0] // SUBLANES, x.shape[1] // LANES))
    return jnp.where(zero == 0, x, 0.0)


def _front_kernel(x_ref, win_ref, sg_ref, sb_ref, ws_ref, bs_ref, h_ref, ya_ref):
    tm = x_ref.shape[0]
    d_sgu = sg_ref.shape[1]
    d_conv = h_ref.shape[1]
    n_chunks = tm // CHUNK
    xb = x_ref[...].astype(jnp.bfloat16)

    u = _dot(xb, win_ref[:, 0:d_sgu])
    v = _dot(xb, win_ref[:, d_sgu:2 * d_sgu])

    lhs = _gated_lhs(xb, x_ref[0:GATE_ROWS, 0:MXU_COLS], v)
    a = _dot(lhs, win_ref[:, 2 * d_sgu:2 * d_sgu + d_conv])
    g = _dot(lhs, win_ref[:, 2 * d_sgu + d_conv:2 * d_sgu + 2 * d_conv])
    h_ref[...] = a * jax.nn.sigmoid(g)

    u = _gelu(u)
    v = _layer_norm(_gelu(v), sg_ref[...], sb_ref[...]).astype(jnp.bfloat16)

    row = lax.broadcasted_iota(jnp.int32, (CHUNK, 2 * CHUNK), 0)
    col = lax.broadcasted_iota(jnp.int32, (CHUNK, 2 * CHUNK), 1)
    causal = (col % CHUNK) <= row
    lane = lax.broadcasted_iota(jnp.int32, (CHUNK, LANES), 1)
    low_head = lane < HEAD_DIM
    zero = jnp.zeros((CHUNK, LANES), jnp.bfloat16)
    for j in range(d_sgu // LANES):
        w_pair = jnp.where(causal, ws_ref[j], jnp.zeros_like(ws_ref[j]))
        bias = bs_ref[:, j * LANES:(j + 1) * LANES]
        for c0 in range(0, n_chunks, 2):
            cs = list(range(c0, min(c0 + 2, n_chunks)))
            rhs = []
            for c in cs:
                vj = v[c * CHUNK:(c + 1) * CHUNK, j * LANES:(j + 1) * LANES]
                rhs.append(jnp.concatenate(
                    [jnp.where(low_head, vj, zero), jnp.where(low_head, zero, vj)], axis=0))
            mixed = _dot(w_pair, jnp.concatenate(rhs, axis=1) if len(rhs) > 1 else rhs[0])
            for i, c in enumerate(cs):
                uc = u[c * CHUNK:(c + 1) * CHUNK, j * LANES:(j + 1) * LANES]
                out_a = uc * (mixed[:, i * LANES:(i + 1) * LANES] + bias)
                ya_ref[c * CHUNK:(c + 1) * CHUNK, j * LANES:(j + 1) * LANES] = out_a.astype(jnp.bfloat16)


def _conv_unit(unit, hext_ref, cw_ref, yconv_ref, rot_ref, after):
    conv_width = cw_ref.shape[0]
    cb, rb = divmod(unit, yconv_ref.shape[0] // CONV_ROWS)
    t0 = rb * CONV_ROWS
    lanes = slice(cb * LANES, (cb + 1) * LANES)
    base = HALO - (conv_width - 1)
    reach = max(o - o % SUBLANES for o in range(base, base + conv_width) if o % SUBLANES)
    lo, hi = (0 if rb == 0 else t0 + reach), t0 + CONV_ROWS + reach
    src = _after(hext_ref[lo:hi + SUBLANES, lanes], after)
    for r in range(1, SUBLANES):
        rot_ref[r - 1, lo:hi, :] = pltpu.roll(src, hi - lo + SUBLANES - r, 0)[0:hi - lo, :]
    acc = None
    for i0 in range(0, CONV_ROWS, ACC_ROWS):
        prev, acc = acc, None
        for o in range(base, base + conv_width):
            r = o % SUBLANES
            start = t0 + i0 + o - r
            if r == 0:
                win = hext_ref[start:start + ACC_ROWS, lanes]
            else:
                win = rot_ref[r - 1, start:start + ACC_ROWS, :]
            term = win * cw_ref[o - base:o - base + 1, lanes]
            if acc is None:
                acc = term if prev is None else _after(term, prev)
            else:
                acc = acc + term
        yconv_ref[t0 + i0:t0 + i0 + ACC_ROWS, lanes] = acc
    return acc


def _paced_layer_norm(x, g_ref, b_ref, o_ref):
    outs = []
    for i, r0 in enumerate(range(0, x.shape[0], LN_ROWS)):
        xi = x[r0:r0 + LN_ROWS, :]
        if i >= LN_DEPTH:
            xi = jnp.concatenate([_after(xi[:, 0:LANES], outs[i - LN_DEPTH]), xi[:, LANES:]], axis=1)
        outs.append(_layer_norm(xi, g_ref[...], b_ref[...]))
        o_ref[r0:r0 + LN_ROWS, :] = outs[-1]
    token = outs[-1][:, 0:LANES]
    for out in outs[-LN_DEPTH:-1]:
        token = _after(token, out)
    return token


def _gated_lhs(lhs, head_f32, token):
    head = _after(head_f32, token).astype(jnp.bfloat16)
    rows = head.shape[0]
    return jnp.concatenate(
        [jnp.concatenate([head, lhs[rows:, 0:MXU_COLS]], axis=0), lhs[:, MXU_COLS:]], axis=1)


def _back_kernel(h_ref, ya_ref, x_ref, cw_ref, cb_ref, cg_ref, cbb_ref, wout_ref, g1_ref, b1_ref,
                 wg_ref, wu_ref, wd_ref, g2_ref, b2_ref, o_ref,
                 x1_ref, hext_ref, yconv_ref, rot_ref, ycat_ref, hid_ref, res_ref, *, tiles_per_seq, alpha):
    tm, d_model = x_ref.shape
    d_conv = h_ref.shape[1]
    d_sgu = ya_ref.shape[1]
    d_ff = wg_ref.shape[1]
    n_chunks = d_ff // FF_CHUNK
    n_units = (tm // CONV_ROWS) * (d_conv // LANES)
    step = pl.program_id(0)

    @pl.when(step == 0)
    def _():
        x1_ref[...] = jnp.zeros_like(x1_ref)
        res_ref[...] = jnp.zeros_like(res_ref)
        hext_ref[tm:tm + HALO, :] = jnp.zeros((HALO, d_conv), jnp.float32)

    out_done = _paced_layer_norm(res_ref[...], g2_ref, b2_ref, o_ref)

    seq_start = (step % tiles_per_seq) == 0
    hext_ref[0:HALO, :] = jnp.where(seq_start, 0.0, hext_ref[tm:tm + HALO, :])
    hext_ref[HALO:HALO + tm, :] = h_ref[...]

    xfb = x1_ref[...].astype(jnp.bfloat16)
    res_ref[...] = alpha * x1_ref[...]
    unit = 0
    done = []
    for ci in range(n_chunks):
        cols = slice(ci * FF_CHUNK, (ci + 1) * FF_CHUNK)
        if ci < LAG:
            lhs = xfb
        else:
            token = done[ci - LAG] if ci != OUT_CHUNK else _after(out_done, done[ci - LAG])
            lhs = _gated_lhs(xfb, x1_ref[0:GATE_ROWS, 0:MXU_COLS], token)
        gate = _dot(lhs, wg_ref[:, cols])
        up = _dot(lhs, wu_ref[:, cols])
        hid_ref[:, cols] = (_silu(gate) * up).astype(jnp.bfloat16)
        token = gate
        for _ in range(-(-(ci + 1) * n_units // n_chunks) - unit):
            token = _conv_unit(unit, hext_ref, cw_ref, yconv_ref, rot_ref, token)
            unit += 1
        done.append(token)

    def down_half(half, token):
        lhs = _gated_lhs(hid_ref[...], hid_ref[0:GATE_ROWS, 0:MXU_COLS].astype(jnp.float32), token)
        for n in range(2 * half, 2 * half + 2):
            cols = slice(n * (d_model // 4), (n + 1) * (d_model // 4))
            res_ref[:, cols] += _dot(lhs, wd_ref[:, cols])

    down_half(0, done[n_chunks - LAG])

    y = yconv_ref[...] + cb_ref[...]
    y = _layer_norm(y, cg_ref[...], cbb_ref[...])
    y = _silu(y)
    ycat_ref[:, 0:d_sgu] = ya_ref[...]
    ycat_ref[:, d_sgu:d_sgu + d_conv] = y.astype(jnp.bfloat16)
    r1 = alpha * x_ref[...] + _dot(ycat_ref[...], wout_ref[...])
    _paced_layer_norm(r1, g1_ref, b1_ref, x1_ref)

    down_half(1, r1)


def _const_spec(shape):
    return pl.BlockSpec(shape, lambda i: (0,) * len(shape), pipeline_mode=pl.Buffered(1))


def _layer(x2d, seq, w_in, sgu_ln_g, sgu_ln_b, w_s, b_s, conv_w, conv_b, conv_ln_g, conv_ln_b,
           w_out, ln1_g, ln1_b, w_gate, w_up, w_down, ln2_g, ln2_b, alpha):
    n_tok, d_model = x2d.shape
    d_sgu = sgu_ln_g.shape[0]
    d_conv = conv_ln_g.shape[0]
    n_heads, chunk, _ = w_s.shape
    conv_width = conv_w.shape[0]
    d_ff = w_gate.shape[1]
    tm = TM
    assert chunk == CHUNK and n_heads * HEAD_DIM == d_sgu and d_sgu % LANES == 0
    assert conv_width - 1 <= HALO and d_ff % FF_CHUNK == 0 and d_model % MXU_COLS == 0
    assert seq % tm == 0 and tm % CHUNK == 0 and tm % CONV_ROWS == 0
    assert seq % CHUNK == 0 and TM_FRONT % CHUNK == 0 and n_tok % TM_FRONT == 0
    n_tiles = n_tok // tm

    bf16 = jnp.bfloat16
    row = lambda p: p.reshape(1, -1)
    ws_pairs = w_s.reshape(n_heads // 2, 2, chunk, chunk).transpose(0, 2, 1, 3).reshape(
        n_heads // 2, chunk, 2 * chunk).astype(bf16)
    bias = jnp.repeat(b_s.T, HEAD_DIM, axis=1)
    params = pltpu.CompilerParams(dimension_semantics=("arbitrary",), vmem_limit_bytes=VMEM_LIMIT_BYTES)

    tile = lambda width: pl.BlockSpec((TM_FRONT, width), lambda i: (i, 0))
    h, ya = pl.pallas_call(
        _front_kernel,
        grid=(n_tok // TM_FRONT,),
        in_specs=[
            tile(d_model),
            _const_spec((d_model, 2 * d_sgu + 2 * d_conv)),
            _const_spec((1, d_sgu)), _const_spec((1, d_sgu)),
            _const_spec((n_heads // 2, chunk, 2 * chunk)),
            _const_spec((chunk, d_sgu)),
        ],
        out_specs=[tile(d_conv), tile(d_sgu)],
        out_shape=[jax.ShapeDtypeStruct((n_tok, d_conv), jnp.float32),
                   jax.ShapeDtypeStruct((n_tok, d_sgu), bf16)],
        compiler_params=params,
        name="front",
    )(x2d, w_in.astype(bf16), row(sgu_ln_g), row(sgu_ln_b), ws_pairs, bias)

    cur = lambda width: pl.BlockSpec((tm, width), lambda i: (jnp.minimum(i, n_tiles - 1), 0))
    prev = pl.BlockSpec((tm, d_model), lambda i: (jnp.maximum(i - 2, 0), 0))
    return pl.pallas_call(
        functools.partial(_back_kernel, tiles_per_seq=seq // tm, alpha=alpha),
        grid=(n_tiles + 2,),
        in_specs=[
            cur(d_conv), cur(d_sgu), cur(d_model),
            _const_spec((conv_width, d_conv)),
            _const_spec((1, d_conv)), _const_spec((1, d_conv)), _const_spec((1, d_conv)),
            _const_spec((d_sgu + d_conv, d_model)),
            _const_spec((1, d_model)), _const_spec((1, d_model)),
            _const_spec((d_model, d_ff)), _const_spec((d_model, d_ff)), _const_spec((d_ff, d_model)),
            _const_spec((1, d_model)), _const_spec((1, d_model)),
        ],
        out_specs=prev,
        out_shape=jax.ShapeDtypeStruct((n_tok, d_model), jnp.float32),
        scratch_shapes=[
            pltpu.VMEM((tm, d_model), jnp.float32),
            pltpu.VMEM((HALO + tm, d_conv), jnp.float32),
            pltpu.VMEM((tm, d_conv), jnp.float32),
            pltpu.VMEM((SUBLANES - 1, HALO + tm, LANES), jnp.float32),
            pltpu.VMEM((tm, d_sgu + d_conv), bf16),
            pltpu.VMEM((tm, d_ff), bf16),
            pltpu.VMEM((tm, d_model), jnp.float32),
        ],
        compiler_params=params,
        name="back",
    )(h, ya, x2d, conv_w, row(conv_b), row(conv_ln_g), row(conv_ln_b), w_out.astype(bf16),
      row(ln1_g), row(ln1_b), w_gate.astype(bf16), w_up.astype(bf16), w_down.astype(bf16),
      row(ln2_g), row(ln2_b))


def kernel(x, w_in, sgu_ln_g, sgu_ln_b, w_s, b_s, conv_w, conv_b, conv_ln_g, conv_ln_b, w_out, ln1_g, ln1_b, w_gate, w_up, w_down, ln2_g, ln2_b):
    batch, seq, d_model = x.shape
    depth = w_in.shape[0]
    alpha = (2.0 * depth) ** 0.25
    x2d = x.reshape(batch * seq, d_model)
    for l in range(depth):
        x2d = _layer(x2d, seq, w_in[l], sgu_ln_g[l], sgu_ln_b[l], w_s[l], b_s[l], conv_w[l], conv_b[l],
                     conv_ln_g[l], conv_ln_b[l], w_out[l], ln1_g[l], ln1_b[l],
                     w_gate[l], w_up[l], w_down[l], ln2_g[l], ln2_b[l], alpha)
    return x2d.reshape(batch, seq, d_model)
```

```python
import functools

import jax
import jax.numpy as jnp
from jax import lax
from jax.experimental import pallas as pl
from jax.experimental.pallas import tpu as pltpu

HEAD_DIM = 64
CHUNK = 128
LN_EPS = 1e-5
LANES = 128
SUBLANES = 8
MXU_COLS = 256
HALO = 32
CONV_ROWS = 32
ACC_ROWS = 16
ROT_PAD = 8
FF_CHUNK = MXU_COLS
GATE_ROWS = 16
LN_ROWS = 8
LN_DEPTH = 8
OUT_CHUNK = 5
LAG = 3
TM = 512
TM_FRONT = 1024
VMEM_LIMIT_BYTES = 56 * 1024 * 1024


def _layer_norm(x, g, b):
    mu = jnp.mean(x, axis=-1, keepdims=True)
    xc = x - mu
    var = jnp.mean(xc * xc, axis=-1, keepdims=True)
    return xc * lax.rsqrt(var + LN_EPS) * g + b


def _gelu(x):
    return 0.5 * x * (1.0 + lax.erf(x * (0.5 ** 0.5)))


def _silu(x):
    hx = 0.5 * x
    return hx + hx * jnp.tanh(hx)


def _dot(a, b):
    return jnp.dot(a, b, preferred_element_type=jnp.float32)


def _zero_bits(dep):
    bits = pltpu.bitcast(dep[0:SUBLANES, 0:LANES], jnp.uint32)
    return lax.shift_right_logical(lax.shift_right_logical(bits, jnp.uint32(16)), jnp.uint32(16))


def _after(x, dep):
    zero = jnp.tile(_zero_bits(dep), (x.shape[0] // SUBLANES, x.shape[1] // LANES))
    return jnp.where(zero == 0, x, 0.0)


def _front_kernel(x_ref, win_ref, sg_ref, sb_ref, ws_ref, bs_ref, h_ref, ya_ref):
    tm = x_ref.shape[0]
    d_sgu = sg_ref.shape[1]
    d_conv = h_ref.shape[1]
    n_chunks = tm // CHUNK
    xb = x_ref[...].astype(jnp.bfloat16)

    u = _dot(xb, win_ref[:, 0:d_sgu])
    v = _dot(xb, win_ref[:, d_sgu:2 * d_sgu])

    lhs = _gated_lhs(xb, x_ref[0:GATE_ROWS, 0:MXU_COLS], v)
    a = _dot(lhs, win_ref[:, 2 * d_sgu:2 * d_sgu + d_conv])
    g = _dot(lhs, win_ref[:, 2 * d_sgu + d_conv:2 * d_sgu + 2 * d_conv])
    h_ref[...] = a * jax.nn.sigmoid(g)

    u = _gelu(u)
    v = _layer_norm(_gelu(v), sg_ref[...], sb_ref[...]).astype(jnp.bfloat16)

    row = lax.broadcasted_iota(jnp.int32, (CHUNK, 2 * CHUNK), 0)
    col = lax.broadcasted_iota(jnp.int32, (CHUNK, 2 * CHUNK), 1)
    causal = (col % CHUNK) <= row
    lane = lax.broadcasted_iota(jnp.int32, (CHUNK, LANES), 1)
    low_head = lane < HEAD_DIM
    zero = jnp.zeros((CHUNK, LANES), jnp.bfloat16)
    for j in range(d_sgu // LANES):
        w_pair = jnp.where(causal, ws_ref[j], jnp.zeros_like(ws_ref[j]))
        bias = bs_ref[:, j * LANES:(j + 1) * LANES]
        for c0 in range(0, n_chunks, 2):
            cs = list(range(c0, min(c0 + 2, n_chunks)))
            rhs = []
            for c in cs:
                vj = v[c * CHUNK:(c + 1) * CHUNK, j * LANES:(j + 1) * LANES]
                rhs.append(jnp.concatenate(
                    [jnp.where(low_head, vj, zero), jnp.where(low_head, zero, vj)], axis=0))
            mixed = _dot(w_pair, jnp.concatenate(rhs, axis=1) if len(rhs) > 1 else rhs[0])
            for i, c in enumerate(cs):
                uc = u[c * CHUNK:(c + 1) * CHUNK, j * LANES:(j + 1) * LANES]
                out_a = uc * (mixed[:, i * LANES:(i + 1) * LANES] + bias)
                ya_ref[c * CHUNK:(c + 1) * CHUNK, j * LANES:(j + 1) * LANES] = out_a.astype(jnp.bfloat16)


def _conv_unit(unit, hext_ref, cw_ref, yconv_ref, rot_ref, after):
    conv_width = cw_ref.shape[0]
    cb, rb = divmod(unit, yconv_ref.shape[0] // CONV_ROWS)
    t0 = rb * CONV_ROWS
    lanes = slice(cb * LANES, (cb + 1) * LANES)
    base = HALO - (conv_width - 1)
    reach = max(o - o % SUBLANES for o in range(base, base + conv_width) if o % SUBLANES)
    lo, hi = (0 if rb == 0 else t0 + reach), t0 + CONV_ROWS + reach
    src = _after(hext_ref[lo:hi + SUBLANES, lanes], after)
    for r in range(1, SUBLANES):
        rot_ref[r - 1, lo:hi, :] = pltpu.roll(src, hi - lo + SUBLANES - r, 0)[0:hi - lo, :]
    acc = None
    for i0 in range(0, CONV_ROWS, ACC_ROWS):
        prev, acc = acc, None
        for o in range(base, base + conv_width):
            r = o % SUBLANES
            start = t0 + i0 + o - r
            if r == 0:
                win = hext_ref[start:start + ACC_ROWS, lanes]
            else:
                win = rot_ref[r - 1, start:start + ACC_ROWS, :]
            term = win * cw_ref[o - base:o - base + 1, lanes]
            if acc is None:
                acc = term if prev is None else _after(term, prev)
            else:
                acc = acc + term
        yconv_ref[t0 + i0:t0 + i0 + ACC_ROWS, lanes] = acc
    return acc


def _paced_layer_norm(x, g_ref, b_ref, o_ref):
    outs = []
    for i, r0 in enumerate(range(0, x.shape[0], LN_ROWS)):
        xi = x[r0:r0 + LN_ROWS, :]
        if i >= LN_DEPTH:
            xi = jnp.concatenate([_after(xi[:, 0:LANES], outs[i - LN_DEPTH]), xi[:, LANES:]], axis=1)
        outs.append(_layer_norm(xi, g_ref[...], b_ref[...]))
        o_ref[r0:r0 + LN_ROWS, :] = outs[-1]
    token = outs[-1][:, 0:LANES]
    for out in outs[-LN_DEPTH:-1]:
        token = _after(token, out)
    return token


def _gated_lhs(lhs, head_f32, token):
    head = _after(head_f32, token).astype(jnp.bfloat16)
    rows = head.shape[0]
    return jnp.concatenate(
        [jnp.concatenate([head, lhs[rows:, 0:MXU_COLS]], axis=0), lhs[:, MXU_COLS:]], axis=1)


def _back_kernel(h_ref, ya_ref, x_ref, cw_ref, cb_ref, cg_ref, cbb_ref, wout_ref, g1_ref, b1_ref,
                 wg_ref, wu_ref, wd_ref, g2_ref, b2_ref, o_ref,
                 x1_ref, hext_ref, yconv_ref, rot_ref, ycat_ref, hid_ref, res_ref, *, tiles_per_seq, alpha):
    tm, d_model = x_ref.shape
    d_conv = h_ref.shape[1]
    d_sgu = ya_ref.shape[1]
    d_ff = wg_ref.shape[1]
    n_chunks = d_ff // FF_CHUNK
    n_units = (tm // CONV_ROWS) * (d_conv // LANES)
    step = pl.program_id(0)

    @pl.when(step == 0)
    def _():
        x1_ref[...] = jnp.zeros_like(x1_ref)
        res_ref[...] = jnp.zeros_like(res_ref)
        hext_ref[tm:tm + HALO, :] = jnp.zeros((HALO, d_conv), jnp.float32)

    out_done = _paced_layer_norm(res_ref[...], g2_ref, b2_ref, o_ref)

    seq_start = (step % tiles_per_seq) == 0
    hext_ref[0:HALO, :] = jnp.where(seq_start, 0.0, hext_ref[tm:tm + HALO, :])
    hext_ref[HALO:HALO + tm, :] = h_ref[...]

    xfb = x1_ref[...].astype(jnp.bfloat16)
    res_ref[...] = alpha * x1_ref[...]
    unit = 0
    done = []
    for ci in range(n_chunks):
        cols = slice(ci * FF_CHUNK, (ci + 1) * FF_CHUNK)
        if ci < LAG:
            lhs = xfb
        else:
            token = done[ci - LAG] if ci != OUT_CHUNK else _after(out_done, done[ci - LAG])
            lhs = _gated_lhs(xfb, x1_ref[0:GATE_ROWS, 0:MXU_COLS], token)
        gate = _dot(lhs, wg_ref[:, cols])
        up = _dot(lhs, wu_ref[:, cols])
        hid_ref[:, cols] = (_silu(gate) * up).astype(jnp.bfloat16)
        token = gate
        for _ in range(-(-(ci + 1) * n_units // n_chunks) - unit):
            token = _conv_unit(unit, hext_ref, cw_ref, yconv_ref, rot_ref, token)
            unit += 1
        done.append(token)

    def down_half(half, token):
        lhs = _gated_lhs(hid_ref[...], hid_ref[0:GATE_ROWS, 0:MXU_COLS].astype(jnp.float32), token)
        for n in range(2 * half, 2 * half + 2):
            cols = slice(n * (d_model // 4), (n + 1) * (d_model // 4))
            res_ref[:, cols] += _dot(lhs, wd_ref[:, cols])

    down_half(0, done[n_chunks - LAG])

    y = yconv_ref[...] + cb_ref[...]
    y = _layer_norm(y, cg_ref[...], cbb_ref[...])
    y = _silu(y)
    ycat_ref[:, 0:d_sgu] = ya_ref[...]
    ycat_ref[:, d_sgu:d_sgu + d_conv] = y.astype(jnp.bfloat16)
    r1 = alpha * x_ref[...] + _dot(ycat_ref[...], wout_ref[...])
    _paced_layer_norm(r1, g1_ref, b1_ref, x1_ref)

    down_half(1, r1)


def _const_spec(shape):
    return pl.BlockSpec(shape, lambda i: (0,) * len(shape), pipeline_mode=pl.Buffered(1))


def _layer(x2d, seq, w_in, sgu_ln_g, sgu_ln_b, w_s, b_s, conv_w, conv_b, conv_ln_g, conv_ln_b,
           w_out, ln1_g, ln1_b, w_gate, w_up, w_down, ln2_g, ln2_b, alpha):
    n_tok, d_model = x2d.shape
    d_sgu = sgu_ln_g.shape[0]
    d_conv = conv_ln_g.shape[0]
    n_heads, chunk, _ = w_s.shape
    conv_width = conv_w.shape[0]
    d_ff = w_gate.shape[1]
    tm = TM
    assert chunk == CHUNK and n_heads * HEAD_DIM == d_sgu and d_sgu % LANES == 0
    assert conv_width - 1 <= HALO and d_ff % FF_CHUNK == 0 and d_model % MXU_COLS == 0
    assert seq % tm == 0 and tm % CHUNK == 0 and tm % CONV_ROWS == 0
    assert seq % CHUNK == 0 and TM_FRONT % CHUNK == 0 and n_tok % TM_FRONT == 0
    n_tiles = n_tok // tm

    bf16 = jnp.bfloat16
    row = lambda p: p.reshape(1, -1)
    ws_pairs = w_s.reshape(n_heads // 2, 2, chunk, chunk).transpose(0, 2, 1, 3).reshape(
        n_heads // 2, chunk, 2 * chunk).astype(bf16)
    bias = jnp.repeat(b_s.T, HEAD_DIM, axis=1)
    params = pltpu.CompilerParams(dimension_semantics=("arbitrary",), vmem_limit_bytes=VMEM_LIMIT_BYTES)

    tile = lambda width: pl.BlockSpec((TM_FRONT, width), lambda i: (i, 0))
    h, ya = pl.pallas_call(
        _front_kernel,
        grid=(n_tok // TM_FRONT,),
        in_specs=[
            tile(d_model),
            _const_spec((d_model, 2 * d_sgu + 2 * d_conv)),
            _const_spec((1, d_sgu)), _const_spec((1, d_sgu)),
            _const_spec((n_heads // 2, chunk, 2 * chunk)),
            _const_spec((chunk, d_sgu)),
        ],
        out_specs=[tile(d_conv), tile(d_sgu)],
        out_shape=[jax.ShapeDtypeStruct((n_tok, d_conv), jnp.float32),
                   jax.ShapeDtypeStruct((n_tok, d_sgu), bf16)],
        compiler_params=params,
        name="front",
    )(x2d, w_in.astype(bf16), row(sgu_ln_g), row(sgu_ln_b), ws_pairs, bias)

    cur = lambda width: pl.BlockSpec((tm, width), lambda i: (jnp.minimum(i, n_tiles - 1), 0))
    prev = pl.BlockSpec((tm, d_model), lambda i: (jnp.maximum(i - 2, 0), 0))
    return pl.pallas_call(
        functools.partial(_back_kernel, tiles_per_seq=seq // tm, alpha=alpha),
        grid=(n_tiles + 2,),
        in_specs=[
            cur(d_conv), cur(d_sgu), cur(d_model),
            _const_spec((conv_width, d_conv)),
            _const_spec((1, d_conv)), _const_spec((1, d_conv)), _const_spec((1, d_conv)),
            _const_spec((d_sgu + d_conv, d_model)),
            _const_spec((1, d_model)), _const_spec((1, d_model)),
            _const_spec((d_model, d_ff)), _const_spec((d_model, d_ff)), _const_spec((d_ff, d_model)),
            _const_spec((1, d_model)), _const_spec((1, d_model)),
        ],
        out_specs=prev,
        out_shape=jax.ShapeDtypeStruct((n_tok, d_model), jnp.float32),
        scratch_shapes=[
            pltpu.VMEM((tm, d_model), jnp.float32),
            pltpu.VMEM((HALO + tm, d_conv), jnp.float32),
            pltpu.VMEM((tm, d_conv), jnp.float32),
            pltpu.VMEM((SUBLANES - 1, HALO + tm + ROT_PAD, LANES), jnp.float32),
            pltpu.VMEM((tm, d_sgu + d_conv), bf16),
            pltpu.VMEM((tm, d_ff), bf16),
            pltpu.VMEM((tm, d_model), jnp.float32),
        ],
        compiler_params=params,
        name="back",
    )(h, ya, x2d, conv_w, row(conv_b), row(conv_ln_g), row(conv_ln_b), w_out.astype(bf16),
      row(ln1_g), row(ln1_b), w_gate.astype(bf16), w_up.astype(bf16), w_down.astype(bf16),
      row(ln2_g), row(ln2_b))


def kernel(x, w_in, sgu_ln_g, sgu_ln_b, w_s, b_s, conv_w, conv_b, conv_ln_g, conv_ln_b, w_out, ln1_g, ln1_b, w_gate, w_up, w_down, ln2_g, ln2_b):
    batch, seq, d_model = x.shape
    depth = w_in.shape[0]
    alpha = (2.0 * depth) ** 0.25
    x2d = x.reshape(batch * seq, d_model)
    for l in range(depth):
        x2d = _layer(x2d, seq, w_in[l], sgu_ln_g[l], sgu_ln_b[l], w_s[l], b_s[l], conv_w[l], conv_b[l],
                     conv_ln_g[l], conv_ln_b[l], w_out[l], ln1_g[l], ln1_b[l],
                     w_gate[l], w_up[l], w_down[l], ln2_g[l], ln2_b[l], alpha)
    return x2d.reshape(batch, seq, d_model)
```

```python
import functools

import jax
import jax.numpy as jnp
from jax import lax
from jax.experimental import pallas as pl
from jax.experimental.pallas import tpu as pltpu

HEAD_DIM = 64
CHUNK = 128
LN_EPS = 1e-5
LANES = 128
SUBLANES = 8
MXU_COLS = 256
HALO = 32
CONV_ROWS = 32
ACC_ROWS = 16
FF_CHUNK = MXU_COLS
GATE_ROWS = 16
LN_ROWS = 8
LN_DEPTH = 8
OUT_CHUNK = 5
LAG = 3
TM = 512
TM_FRONT = 1024
VMEM_LIMIT_BYTES = 56 * 1024 * 1024


def _layer_norm(x, g, b):
    mu = jnp.mean(x, axis=-1, keepdims=True)
    xc = x - mu
    var = jnp.mean(xc * xc, axis=-1, keepdims=True)
    return xc * lax.rsqrt(var + LN_EPS) * g + b


def _gelu(x):
    return 0.5 * x * (1.0 + lax.erf(x * (0.5 ** 0.5)))


def _silu(x):
    hx = 0.5 * x
    return hx + hx * jnp.tanh(hx)


def _dot(a, b):
    return jnp.dot(a, b, preferred_element_type=jnp.float32)


def _zero_bits(dep):
    bits = pltpu.bitcast(dep[0:SUBLANES, 0:LANES], jnp.uint32)
    return lax.shift_right_logical(lax.shift_right_logical(bits, jnp.uint32(16)), jnp.uint32(16))


def _after(x, dep):
    zero = jnp.tile(_zero_bits(dep), (x.shape[0] // SUBLANES, x.shape[1] // LANES))
    return jnp.where(zero == 0, x, 0.0)


def _front_kernel(x_ref, win_ref, sg_ref, sb_ref, ws_ref, bs_ref, h_ref, ya_ref):
    tm = x_ref.shape[0]
    d_sgu = sg_ref.shape[1]
    d_conv = h_ref.shape[1]
    n_chunks = tm // CHUNK
    xb = x_ref[...].astype(jnp.bfloat16)

    u = _dot(xb, win_ref[:, 0:d_sgu])
    v = _dot(xb, win_ref[:, d_sgu:2 * d_sgu])

    lhs = _gated_lhs(xb, x_ref[0:GATE_ROWS, 0:MXU_COLS], v)
    a = _dot(lhs, win_ref[:, 2 * d_sgu:2 * d_sgu + d_conv])
    g = _dot(lhs, win_ref[:, 2 * d_sgu + d_conv:2 * d_sgu + 2 * d_conv])
    h_ref[...] = a * jax.nn.sigmoid(g)

    u = _gelu(u)
    v = _layer_norm(_gelu(v), sg_ref[...], sb_ref[...]).astype(jnp.bfloat16)

    row = lax.broadcasted_iota(jnp.int32, (CHUNK, 2 * CHUNK), 0)
    col = lax.broadcasted_iota(jnp.int32, (CHUNK, 2 * CHUNK), 1)
    causal = (col % CHUNK) <= row
    lane = lax.broadcasted_iota(jnp.int32, (CHUNK, LANES), 1)
    low_head = lane < HEAD_DIM
    zero = jnp.zeros((CHUNK, LANES), jnp.bfloat16)
    for j in range(d_sgu // LANES):
        w_pair = jnp.where(causal, ws_ref[j], jnp.zeros_like(ws_ref[j]))
        bias = bs_ref[:, j * LANES:(j + 1) * LANES]
        for c0 in range(0, n_chunks, 2):
            cs = list(range(c0, min(c0 + 2, n_chunks)))
            rhs = []
            for c in cs:
                vj = v[c * CHUNK:(c + 1) * CHUNK, j * LANES:(j + 1) * LANES]
                rhs.append(jnp.concatenate(
                    [jnp.where(low_head, vj, zero), jnp.where(low_head, zero, vj)], axis=0))
            mixed = _dot(w_pair, jnp.concatenate(rhs, axis=1) if len(rhs) > 1 else rhs[0])
            for i, c in enumerate(cs):
                uc = u[c * CHUNK:(c + 1) * CHUNK, j * LANES:(j + 1) * LANES]
                out_a = uc * (mixed[:, i * LANES:(i + 1) * LANES] + bias)
                ya_ref[c * CHUNK:(c + 1) * CHUNK, j * LANES:(j + 1) * LANES] = out_a.astype(jnp.bfloat16)


def _conv_unit(unit, hext_ref, cw_ref, yconv_ref, rot_ref, after):
    conv_width = cw_ref.shape[0]
    cb, rb = divmod(unit, yconv_ref.shape[0] // CONV_ROWS)
    t0 = rb * CONV_ROWS
    lanes = slice(cb * LANES, (cb + 1) * LANES)
    base = HALO - (conv_width - 1)
    reach = max(o - o % SUBLANES for o in range(base, base + conv_width) if o % SUBLANES)
    lo, hi = (0 if rb == 0 else t0 + reach), t0 + CONV_ROWS + reach
    src = _after(hext_ref[lo:hi + SUBLANES, lanes], after)
    for r in range(1, SUBLANES):
        rot_ref[r - 1, lo:hi, :] = pltpu.roll(src, hi - lo + SUBLANES - r, 0)[0:hi - lo, :]
    acc = None
    for i0 in range(0, CONV_ROWS, ACC_ROWS):
        prev, acc = acc, None
        for o in range(base, base + conv_width):
            r = o % SUBLANES
            start = t0 + i0 + o - r
            if r == 0:
                win = hext_ref[start:start + ACC_ROWS, lanes]
            else:
                win = rot_ref[r - 1, start:start + ACC_ROWS, :]
            term = win * cw_ref[o - base:o - base + 1, lanes]
            if acc is None:
                acc = term if prev is None else _after(term, prev)
            else:
                acc = acc + term
        yconv_ref[t0 + i0:t0 + i0 + ACC_ROWS, lanes] = acc
    return acc


def _paced_layer_norm(x, g_ref, b_ref, o_ref):
    outs = []
    for i, r0 in enumerate(range(0, x.shape[0], LN_ROWS)):
        xi = x[r0:r0 + LN_ROWS, :]
        if i >= LN_DEPTH:
            xi = jnp.concatenate([_after(xi[:, 0:LANES], outs[i - LN_DEPTH]), xi[:, LANES:]], axis=1)
        outs.append(_layer_norm(xi, g_ref[...], b_ref[...]))
        o_ref[r0:r0 + LN_ROWS, :] = outs[-1]
    token = outs[-1][:, 0:LANES]
    for out in outs[-LN_DEPTH:-1]:
        token = _after(token, out)
    return token


def _gated_lhs(lhs, head_f32, token):
    head = _after(head_f32, token).astype(jnp.bfloat16)
    rows = head.shape[0]
    return jnp.concatenate(
        [jnp.concatenate([head, lhs[rows:, 0:MXU_COLS]], axis=0), lhs[:, MXU_COLS:]], axis=1)


def _back_kernel(h_ref, ya_ref, x_ref, cw_ref, cb_ref, cg_ref, cbb_ref, wout_ref, g1_ref, b1_ref,
                 wg_ref, wu_ref, wd_ref, g2_ref, b2_ref, o_ref,
                 x1_ref, hext_ref, yconv_ref, rot_ref, ycat_ref, hid_ref, res_ref, *, tiles_per_seq, alpha):
    tm, d_model = x_ref.shape
    d_conv = h_ref.shape[1]
    d_sgu = ya_ref.shape[1]
    d_ff = wg_ref.shape[1]
    n_chunks = d_ff // FF_CHUNK
    n_units = (tm // CONV_ROWS) * (d_conv // LANES)
    step = pl.program_id(0)

    @pl.when(step == 0)
    def _():
        x1_ref[...] = jnp.zeros_like(x1_ref)
        res_ref[...] = jnp.zeros_like(res_ref)
        hext_ref[tm:tm + HALO, :] = jnp.zeros((HALO, d_conv), jnp.float32)

    out_done = _paced_layer_norm(res_ref[...], g2_ref, b2_ref, o_ref)

    seq_start = (step % tiles_per_seq) == 0
    hext_ref[0:HALO, :] = jnp.where(seq_start, 0.0, hext_ref[tm:tm + HALO, :])
    hext_ref[HALO:HALO + tm, :] = h_ref[...]

    xfb = x1_ref[...].astype(jnp.bfloat16)
    res_ref[...] = alpha * x1_ref[...]
    unit = 0
    done = []
    for ci in range(n_chunks):
        cols = slice(ci * FF_CHUNK, (ci + 1) * FF_CHUNK)
        if ci < LAG:
            lhs = xfb
        else:
            token = done[ci - LAG] if ci != OUT_CHUNK else _after(out_done, done[ci - LAG])
            lhs = _gated_lhs(xfb, x1_ref[0:GATE_ROWS, 0:MXU_COLS], token)
        gate = _dot(lhs, wg_ref[:, cols])
        up = _dot(lhs, wu_ref[:, cols])
        hid_ref[:, cols] = (_silu(gate) * up).astype(jnp.bfloat16)
        token = gate
        for _ in range(-(-(ci + 1) * n_units // n_chunks) - unit):
            token = _conv_unit(unit, hext_ref, cw_ref, yconv_ref, rot_ref, token)
            unit += 1
        done.append(token)

    def down_half(half, token):
        lhs = _gated_lhs(hid_ref[...], hid_ref[0:GATE_ROWS, 0:MXU_COLS].astype(jnp.float32), token)
        for n in range(2 * half, 2 * half + 2):
            cols = slice(n * (d_model // 4), (n + 1) * (d_model // 4))
            res_ref[:, cols] += _dot(lhs, wd_ref[:, cols])

    down_half(0, done[n_chunks - LAG])

    y = yconv_ref[...] + cb_ref[...]
    y = _layer_norm(y, cg_ref[...], cbb_ref[...])
    y = _silu(y)
    ycat_ref[:, 0:d_sgu] = ya_ref[...]
    ycat_ref[:, d_sgu:d_sgu + d_conv] = y.astype(jnp.bfloat16)
    r1 = alpha * x_ref[...] + _dot(ycat_ref[...], wout_ref[...])
    _paced_layer_norm(r1, g1_ref, b1_ref, x1_ref)

    down_half(1, r1)


def _const_spec(shape):
    return pl.BlockSpec(shape, lambda i: (0,) * len(shape), pipeline_mode=pl.Buffered(1))


def _layer(x2d, seq, w_in, sgu_ln_g, sgu_ln_b, w_s, b_s, conv_w, conv_b, conv_ln_g, conv_ln_b,
           w_out, ln1_g, ln1_b, w_gate, w_up, w_down, ln2_g, ln2_b, alpha):
    n_tok, d_model = x2d.shape
    d_sgu = sgu_ln_g.shape[0]
    d_conv = conv_ln_g.shape[0]
    n_heads, chunk, _ = w_s.shape
    conv_width = conv_w.shape[0]
    d_ff = w_gate.shape[1]
    tm = TM
    assert chunk == CHUNK and n_heads * HEAD_DIM == d_sgu and d_sgu % LANES == 0
    assert conv_width - 1 <= HALO and d_ff % FF_CHUNK == 0 and d_model % MXU_COLS == 0
    assert seq % tm == 0 and tm % CHUNK == 0 and tm % CONV_ROWS == 0
    assert seq % CHUNK == 0 and TM_FRONT % CHUNK == 0 and n_tok % TM_FRONT == 0
    n_tiles = n_tok // tm

    bf16 = jnp.bfloat16
    row = lambda p: p.reshape(1, -1)
    ws_pairs = w_s.reshape(n_heads // 2, 2, chunk, chunk).transpose(0, 2, 1, 3).reshape(
        n_heads // 2, chunk, 2 * chunk).astype(bf16)
    bias = jnp.repeat(b_s.T, HEAD_DIM, axis=1)
    params = pltpu.CompilerParams(dimension_semantics=("arbitrary",), vmem_limit_bytes=VMEM_LIMIT_BYTES)

    tile = lambda width: pl.BlockSpec((TM_FRONT, width), lambda i: (i, 0))
    h, ya = pl.pallas_call(
        _front_kernel,
        grid=(n_tok // TM_FRONT,),
        in_specs=[
            tile(d_model),
            _const_spec((d_model, 2 * d_sgu + 2 * d_conv)),
            _const_spec((1, d_sgu)), _const_spec((1, d_sgu)),
            _const_spec((n_heads // 2, chunk, 2 * chunk)),
            _const_spec((chunk, d_sgu)),
        ],
        out_specs=[tile(d_conv), tile(d_sgu)],
        out_shape=[jax.ShapeDtypeStruct((n_tok, d_conv), jnp.float32),
                   jax.ShapeDtypeStruct((n_tok, d_sgu), bf16)],
        compiler_params=params,
        name="front",
    )(x2d, w_in.astype(bf16), row(sgu_ln_g), row(sgu_ln_b), ws_pairs, bias)

    cur = lambda width: pl.BlockSpec((tm, width), lambda i: (jnp.minimum(i, n_tiles - 1), 0))
    prev = pl.BlockSpec((tm, d_model), lambda i: (jnp.maximum(i - 2, 0), 0))
    return pl.pallas_call(
        functools.partial(_back_kernel, tiles_per_seq=seq // tm, alpha=alpha),
        grid=(n_tiles + 2,),
        in_specs=[
            cur(d_conv), cur(d_sgu), cur(d_model),
            _const_spec((conv_width, d_conv)),
            _const_spec((1, d_conv)), _const_spec((1, d_conv)), _const_spec((1, d_conv)),
            _const_spec((d_sgu + d_conv, d_model)),
            _const_spec((1, d_model)), _const_spec((1, d_model)),
            _const_spec((d_model, d_ff)), _const_spec((d_model, d_ff)), _const_spec((d_ff, d_model)),
            _const_spec((1, d_model)), _const_spec((1, d_model)),
        ],
        out_specs=prev,
        out_shape=jax.ShapeDtypeStruct((n_tok, d_model), jnp.float32),
        scratch_shapes=[
            pltpu.VMEM((tm, d_model), jnp.float32),
            pltpu.VMEM((HALO + tm, d_conv), jnp.float32),
            pltpu.VMEM((tm, d_conv), jnp.float32),
            pltpu.VMEM((SUBLANES - 1, HALO + tm, LANES), jnp.float32),
            pltpu.VMEM((tm, d_sgu + d_conv), bf16),
            pltpu.VMEM((tm, d_ff), bf16),
            pltpu.VMEM((tm, d_model), jnp.float32),
        ],
        compiler_params=params,
        name="back",
    )(h, ya, x2d, conv_w, row(conv_b), row(conv_ln_g), row(conv_ln_b), w_out.astype(bf16),
      row(ln1_g), row(ln1_b), w_gate.astype(bf16), w_up.astype(bf16), w_down.astype(bf16),
      row(ln2_g), row(ln2_b))


def kernel(x, w_in, sgu_ln_g, sgu_ln_b, w_s, b_s, conv_w, conv_b, conv_ln_g, conv_ln_b, w_out, ln1_g, ln1_b, w_gate, w_up, w_down, ln2_g, ln2_b):
    batch, seq, d_model = x.shape
    depth = w_in.shape[0]
    alpha = (2.0 * depth) ** 0.25
    x2d = x.reshape(batch * seq, d_model)
    for l in range(depth):
        x2d = _layer(x2d, seq, w_in[l], sgu_ln_g[l], sgu_ln_b[l], w_s[l], b_s[l], conv_w[l], conv_b[l],
                     conv_ln_g[l], conv_ln_b[l], w_out[l], ln1_g[l], ln1_b[l],
                     w_gate[l], w_up[l], w_down[l], ln2_g[l], ln2_b[l], alpha)
    return x2d.reshape(batch, seq, d_model)
```

```python
import functools

import jax
import jax.numpy as jnp
from jax import lax
from jax.experimental import pallas as pl
from jax.experimental.pallas import tpu as pltpu

HEAD_DIM = 64
CHUNK = 128
LN_EPS = 1e-5
LANES = 128
SUBLANES = 8
MXU_COLS = 256
HALO = 32
CONV_ROWS = 32
ACC_ROWS = 16
FF_CHUNK = MXU_COLS
GATE_ROWS = 16
LN_ROWS = 8
LN_DEPTH = 8
OUT_CHUNK = 5
LAG = 3
TM = 512
TM_FRONT = 1024
VMEM_LIMIT_BYTES = 56 * 1024 * 1024


def _layer_norm(x, g, b):
    mu = jnp.mean(x, axis=-1, keepdims=True)
    xc = x - mu
    var = jnp.mean(xc * xc, axis=-1, keepdims=True)
    return xc * lax.rsqrt(var + LN_EPS) * g + b


def _gelu(x):
    return 0.5 * x * (1.0 + lax.erf(x * (0.5 ** 0.5)))


def _silu(x):
    hx = 0.5 * x
    return hx + hx * jnp.tanh(hx)


def _dot(a, b):
    return jnp.dot(a, b, preferred_element_type=jnp.float32)


def _zero_bits(dep):
    bits = pltpu.bitcast(dep[0:SUBLANES, 0:LANES], jnp.uint32)
    return lax.shift_right_logical(lax.shift_right_logical(bits, jnp.uint32(16)), jnp.uint32(16))


def _after(x, dep):
    zero = jnp.tile(_zero_bits(dep), (x.shape[0] // SUBLANES, x.shape[1] // LANES))
    return jnp.where(zero == 0, x, 0.0)


def _front_kernel(x_ref, win_ref, sg_ref, sb_ref, ws_ref, bs_ref, h_ref, ya_ref):
    tm = x_ref.shape[0]
    d_sgu = sg_ref.shape[1]
    d_conv = h_ref.shape[1]
    n_chunks = tm // CHUNK
    xb = x_ref[...].astype(jnp.bfloat16)

    u = _dot(xb, win_ref[:, 0:d_sgu])
    v = _dot(xb, win_ref[:, d_sgu:2 * d_sgu])

    lhs = _gated_lhs(xb, x_ref[0:GATE_ROWS, 0:MXU_COLS], v)
    a = _dot(lhs, win_ref[:, 2 * d_sgu:2 * d_sgu + d_conv])
    g = _dot(lhs, win_ref[:, 2 * d_sgu + d_conv:2 * d_sgu + 2 * d_conv])
    h_ref[...] = a * jax.nn.sigmoid(g)

    u = _gelu(u)
    v = _layer_norm(_gelu(v), sg_ref[...], sb_ref[...]).astype(jnp.bfloat16)

    row = lax.broadcasted_iota(jnp.int32, (CHUNK, 2 * CHUNK), 0)
    col = lax.broadcasted_iota(jnp.int32, (CHUNK, 2 * CHUNK), 1)
    causal = (col % CHUNK) <= row
    lane = lax.broadcasted_iota(jnp.int32, (CHUNK, LANES), 1)
    low_head = lane < HEAD_DIM
    zero = jnp.zeros((CHUNK, LANES), jnp.bfloat16)
    for j in range(d_sgu // LANES):
        w_pair = jnp.where(causal, ws_ref[j], jnp.zeros_like(ws_ref[j]))
        bias = bs_ref[:, j * LANES:(j + 1) * LANES]
        for c0 in range(0, n_chunks, 2):
            cs = list(range(c0, min(c0 + 2, n_chunks)))
            rhs = []
            for c in cs:
                vj = v[c * CHUNK:(c + 1) * CHUNK, j * LANES:(j + 1) * LANES]
                rhs.append(jnp.concatenate(
                    [jnp.where(low_head, vj, zero), jnp.where(low_head, zero, vj)], axis=0))
            mixed = _dot(w_pair, jnp.concatenate(rhs, axis=1) if len(rhs) > 1 else rhs[0])
            for i, c in enumerate(cs):
                uc = u[c * CHUNK:(c + 1) * CHUNK, j * LANES:(j + 1) * LANES]
                out_a = uc * (mixed[:, i * LANES:(i + 1) * LANES] + bias)
                ya_ref[c * CHUNK:(c + 1) * CHUNK, j * LANES:(j + 1) * LANES] = out_a.astype(jnp.bfloat16)


def _conv_unit(unit, hext_ref, cw_ref, yconv_ref, rot_ref, after):
    conv_width = cw_ref.shape[0]
    cb, rb = divmod(unit, yconv_ref.shape[0] // CONV_ROWS)
    t0 = rb * CONV_ROWS
    lanes = slice(cb * LANES, (cb + 1) * LANES)
    base = HALO - (conv_width - 1)
    reach = max(o - o % SUBLANES for o in range(base, base + conv_width) if o % SUBLANES)
    lo, hi = (0 if rb == 0 else t0 + reach), t0 + CONV_ROWS + reach
    src = _after(hext_ref[lo:hi + SUBLANES, lanes], after)
    for r in range(1, SUBLANES):
        rot_ref[r - 1, lo:hi, :] = pltpu.roll(src, hi - lo + SUBLANES - r, 0)[0:hi - lo, :]
    acc = None
    for i0 in range(0, CONV_ROWS, ACC_ROWS):
        prev, acc = acc, None
        for o in range(base, base + conv_width):
            r = o % SUBLANES
            start = t0 + i0 + o - r
            if r == 0:
                win = hext_ref[start:start + ACC_ROWS, lanes]
            else:
                win = rot_ref[r - 1, start:start + ACC_ROWS, :]
            term = win * cw_ref[o - base:o - base + 1, lanes]
            if acc is None:
                acc = term if prev is None else _after(term, prev)
            else:
                acc = acc + term
        yconv_ref[t0 + i0:t0 + i0 + ACC_ROWS, lanes] = acc
    return acc


def _paced_layer_norm(x, g_ref, b_ref, o_ref):
    outs = []
    for i, r0 in enumerate(range(0, x.shape[0], LN_ROWS)):
        xi = x[r0:r0 + LN_ROWS, :]
        if i >= LN_DEPTH:
            xi = jnp.concatenate([_after(xi[:, 0:LANES], outs[i - LN_DEPTH]), xi[:, LANES:]], axis=1)
        outs.append(_layer_norm(xi, g_ref[...], b_ref[...]))
        o_ref[r0:r0 + LN_ROWS, :] = outs[-1]
    token = outs[-1][:, 0:LANES]
    for out in outs[-LN_DEPTH:-1]:
        token = _after(token, out)
    return token


def _gated_lhs(lhs, head_f32, token):
    head = _after(head_f32, token).astype(jnp.bfloat16)
    rows = head.shape[0]
    return jnp.concatenate(
        [jnp.concatenate([head, lhs[rows:, 0:MXU_COLS]], axis=0), lhs[:, MXU_COLS:]], axis=1)


def _back_kernel(h_ref, ya_ref, x_ref, cw_ref, cb_ref, cg_ref, cbb_ref, wout_ref, g1_ref, b1_ref,
                 wg_ref, wu_ref, wd_ref, g2_ref, b2_ref, o_ref,
                 x1_ref, xfb_ref, hext_ref, yconv_ref, rot_ref, ycat_ref, hid_ref, res_ref,
                 *, tiles_per_seq, alpha):
    tm, d_model = x_ref.shape
    d_conv = h_ref.shape[1]
    d_sgu = ya_ref.shape[1]
    d_ff = wg_ref.shape[1]
    n_chunks = d_ff // FF_CHUNK
    n_units = (tm // CONV_ROWS) * (d_conv // LANES)
    step = pl.program_id(0)

    @pl.when(step == 0)
    def _():
        x1_ref[...] = jnp.zeros_like(x1_ref)
        res_ref[...] = jnp.zeros_like(res_ref)
        hext_ref[tm:tm + HALO, :] = jnp.zeros((HALO, d_conv), jnp.float32)

    out_done = _paced_layer_norm(res_ref[...], g2_ref, b2_ref, o_ref)

    seq_start = (step % tiles_per_seq) == 0
    hext_ref[0:HALO, :] = jnp.where(seq_start, 0.0, hext_ref[tm:tm + HALO, :])
    hext_ref[HALO:HALO + tm, :] = h_ref[...]

    xfb_ref[...] = x1_ref[...].astype(jnp.bfloat16)
    res_ref[...] = alpha * x1_ref[...]
    unit = 0
    done = []
    for ci in range(n_chunks):
        cols = slice(ci * FF_CHUNK, (ci + 1) * FF_CHUNK)
        xfb = xfb_ref[...]
        if ci < LAG:
            lhs = xfb
        else:
            token = done[ci - LAG] if ci != OUT_CHUNK else _after(out_done, done[ci - LAG])
            lhs = _gated_lhs(xfb, x1_ref[0:GATE_ROWS, 0:MXU_COLS], token)
        gate = _dot(lhs, wg_ref[:, cols])
        up = _dot(lhs, wu_ref[:, cols])
        hid_ref[:, cols] = (_silu(gate) * up).astype(jnp.bfloat16)
        token = gate
        for _ in range(-(-(ci + 1) * n_units // n_chunks) - unit):
            token = _conv_unit(unit, hext_ref, cw_ref, yconv_ref, rot_ref, token)
            unit += 1
        done.append(token)

    def down_half(half, token):
        lhs = _gated_lhs(hid_ref[...], hid_ref[0:GATE_ROWS, 0:MXU_COLS].astype(jnp.float32), token)
        for n in range(2 * half, 2 * half + 2):
            cols = slice(n * (d_model // 4), (n + 1) * (d_model // 4))
            res_ref[:, cols] += _dot(lhs, wd_ref[:, cols])

    down_half(0, done[n_chunks - LAG])

    y = yconv_ref[...] + cb_ref[...]
    y = _layer_norm(y, cg_ref[...], cbb_ref[...])
    y = _silu(y)
    ycat_ref[:, 0:d_sgu] = ya_ref[...]
    ycat_ref[:, d_sgu:d_sgu + d_conv] = y.astype(jnp.bfloat16)
    r1 = alpha * x_ref[...] + _dot(ycat_ref[...], wout_ref[...])
    _paced_layer_norm(r1, g1_ref, b1_ref, x1_ref)

    down_half(1, r1)


def _const_spec(shape):
    return pl.BlockSpec(shape, lambda i: (0,) * len(shape), pipeline_mode=pl.Buffered(1))


def _layer(x2d, seq, w_in, sgu_ln_g, sgu_ln_b, w_s, b_s, conv_w, conv_b, conv_ln_g, conv_ln_b,
           w_out, ln1_g, ln1_b, w_gate, w_up, w_down, ln2_g, ln2_b, alpha):
    n_tok, d_model = x2d.shape
    d_sgu = sgu_ln_g.shape[0]
    d_conv = conv_ln_g.shape[0]
    n_heads, chunk, _ = w_s.shape
    conv_width = conv_w.shape[0]
    d_ff = w_gate.shape[1]
    tm = TM
    assert chunk == CHUNK and n_heads * HEAD_DIM == d_sgu and d_sgu % LANES == 0
    assert conv_width - 1 <= HALO and d_ff % FF_CHUNK == 0 and d_model % MXU_COLS == 0
    assert seq % tm == 0 and tm % CHUNK == 0 and tm % CONV_ROWS == 0
    assert seq % CHUNK == 0 and TM_FRONT % CHUNK == 0 and n_tok % TM_FRONT == 0
    n_tiles = n_tok // tm

    bf16 = jnp.bfloat16
    row = lambda p: p.reshape(1, -1)
    ws_pairs = w_s.reshape(n_heads // 2, 2, chunk, chunk).transpose(0, 2, 1, 3).reshape(
        n_heads // 2, chunk, 2 * chunk).astype(bf16)
    bias = jnp.repeat(b_s.T, HEAD_DIM, axis=1)
    params = pltpu.CompilerParams(dimension_semantics=("arbitrary",), vmem_limit_bytes=VMEM_LIMIT_BYTES)

    tile = lambda width: pl.BlockSpec((TM_FRONT, width), lambda i: (i, 0))
    h, ya = pl.pallas_call(
        _front_kernel,
        grid=(n_tok // TM_FRONT,),
        in_specs=[
            tile(d_model),
            _const_spec((d_model, 2 * d_sgu + 2 * d_conv)),
            _const_spec((1, d_sgu)), _const_spec((1, d_sgu)),
            _const_spec((n_heads // 2, chunk, 2 * chunk)),
            _const_spec((chunk, d_sgu)),
        ],
        out_specs=[tile(d_conv), tile(d_sgu)],
        out_shape=[jax.ShapeDtypeStruct((n_tok, d_conv), jnp.float32),
                   jax.ShapeDtypeStruct((n_tok, d_sgu), bf16)],
        compiler_params=params,
        name="front",
    )(x2d, w_in.astype(bf16), row(sgu_ln_g), row(sgu_ln_b), ws_pairs, bias)

    cur = lambda width: pl.BlockSpec((tm, width), lambda i: (jnp.minimum(i, n_tiles - 1), 0))
    prev = pl.BlockSpec((tm, d_model), lambda i: (jnp.maximum(i - 2, 0), 0))
    return pl.pallas_call(
        functools.partial(_back_kernel, tiles_per_seq=seq // tm, alpha=alpha),
        grid=(n_tiles + 2,),
        in_specs=[
            cur(d_conv), cur(d_sgu), cur(d_model),
            _const_spec((conv_width, d_conv)),
            _const_spec((1, d_conv)), _const_spec((1, d_conv)), _const_spec((1, d_conv)),
            _const_spec((d_sgu + d_conv, d_model)),
            _const_spec((1, d_model)), _const_spec((1, d_model)),
            _const_spec((d_model, d_ff)), _const_spec((d_model, d_ff)), _const_spec((d_ff, d_model)),
            _const_spec((1, d_model)), _const_spec((1, d_model)),
        ],
        out_specs=prev,
        out_shape=jax.ShapeDtypeStruct((n_tok, d_model), jnp.float32),
        scratch_shapes=[
            pltpu.VMEM((tm, d_model), jnp.float32),
            pltpu.VMEM((tm, d_model), bf16),
            pltpu.VMEM((HALO + tm, d_conv), jnp.float32),
            pltpu.VMEM((tm, d_conv), jnp.float32),
            pltpu.VMEM((SUBLANES - 1, HALO + tm, LANES), jnp.float32),
            pltpu.VMEM((tm, d_sgu + d_conv), bf16),
            pltpu.VMEM((tm, d_ff), bf16),
            pltpu.VMEM((tm, d_model), jnp.float32),
        ],
        compiler_params=params,
        name="back",
    )(h, ya, x2d, conv_w, row(conv_b), row(conv_ln_g), row(conv_ln_b), w_out.astype(bf16),
      row(ln1_g), row(ln1_b), w_gate.astype(bf16), w_up.astype(bf16), w_down.astype(bf16),
      row(ln2_g), row(ln2_b))


def kernel(x, w_in, sgu_ln_g, sgu_ln_b, w_s, b_s, conv_w, conv_b, conv_ln_g, conv_ln_b, w_out, ln1_g, ln1_b, w_gate, w_up, w_down, ln2_g, ln2_b):
    batch, seq, d_model = x.shape
    depth = w_in.shape[0]
    alpha = (2.0 * depth) ** 0.25
    x2d = x.reshape(batch * seq, d_model)
    for l in range(depth):
        x2d = _layer(x2d, seq, w_in[l], sgu_ln_g[l], sgu_ln_b[l], w_s[l], b_s[l], conv_w[l], conv_b[l],
                     conv_ln_g[l], conv_ln_b[l], w_out[l], ln1_g[l], ln1_b[l],
                     w_gate[l], w_up[l], w_down[l], ln2_g[l], ln2_b[l], alpha)
    return x2d.reshape(batch, seq, d_model)
```
